```python
import jax
import jax.numpy as jnp
from jax import lax
import numpy as np

D_MODEL = 1024
BATCH = 8
SEQ = 4096
DEPTH = 2
DEC_BATCH = 16
DEC_SEQ = 16
PAST_LEN = 4096

CHUNK = 64
HEAD_DIM = 64
SB_WIDTH = 3 * D_MODEL // 4
N_SB_HEADS = SB_WIDTH // HEAD_DIM
SB_BLOCK = 128
SB_SCALE = HEAD_DIM ** -0.5
SGU_WIDTH = 3 * D_MODEL // 4
SGU_GROUP_DIM = 64
N_SGU_GROUPS = SGU_WIDTH // SGU_GROUP_DIM
MLP_CHUNK = 128
N_MEM = 256
MEM_WIDTH = D_MODEL // 4
N_MEM_HEADS = MEM_WIDTH // HEAD_DIM
D_FF = -(-8 * D_MODEL // (3 * 256)) * 256
N_A_LAYERS = (DEPTH + 1) // 2
N_B_LAYERS = DEPTH // 2
EPS = 1e-6

kernel_name = "stickbreak_sgu_hybrid_stream_step"


def rmsnorm(x, g):
    xf = x.astype(jnp.float32)
    y = xf * lax.rsqrt(jnp.mean(xf * xf, axis=-1, keepdims=True) + EPS)
    return (y * g.astype(jnp.float32)).astype(x.dtype)


def split_heads(x, n_heads):
    return x.reshape(x.shape[:-1] + (n_heads, HEAD_DIM))


def stick_breaking(q, k, v, q_pos, k_pos):
    z = jnp.einsum('bqhd,bkhd->bhqk', q, k, preferred_element_type=jnp.float32) * SB_SCALE
    causal = k_pos[None, :] < q_pos[:, None]
    log_keep = jnp.where(causal, jax.nn.log_sigmoid(-z), 0.0)
    log_rest = lax.cumsum(log_keep, axis=3, reverse=True) - log_keep
    a = jnp.where(causal, jnp.exp(jax.nn.log_sigmoid(z) + log_rest), 0.0)
    return jnp.einsum('bhqk,bkhd->bqhd', a.astype(v.dtype), v)


def stick_breaking_mixer(h, w_in, cache_k=None, cache_v=None):
    proj = h @ w_in
    q, k, v, q_mem = jnp.split(proj, [SB_WIDTH, 2 * SB_WIDTH, 3 * SB_WIDTH], axis=-1)
    q, k, v = split_heads(q, N_SB_HEADS), split_heads(k, N_SB_HEADS), split_heads(v, N_SB_HEADS)
    t_len = h.shape[1]
    if cache_k is None:
        pos = jnp.arange(t_len)
        outs = []
        for start in range(0, t_len, SB_BLOCK):
            stop = min(start + SB_BLOCK, t_len)
            outs.append(stick_breaking(q[:, start:stop], k[:, :stop], v[:, :stop],
                                       pos[start:stop], pos[:stop]))
        o = jnp.concatenate(outs, axis=1)
    else:
        past = cache_k.shape[1]
        k_all = jnp.concatenate([cache_k, k], axis=1)
        v_all = jnp.concatenate([cache_v, v], axis=1)
        o = stick_breaking(q, k_all, v_all, past + jnp.arange(t_len), jnp.arange(past + t_len))
    return o.reshape(o.shape[:2] + (SB_WIDTH,)), q_mem, k, v


def spatial_gating_mixer(h, w_in, w_sp, b_sp, g_sgu):
    proj = h @ w_in
    u, v, q_mem = jnp.split(proj, [SGU_WIDTH, 2 * SGU_WIDTH], axis=-1)
    u = jax.nn.gelu(u)
    v = rmsnorm(jax.nn.gelu(v), g_sgu)
    bsz, t_len, _ = h.shape
    span = min(t_len, MLP_CHUNK)
    tri = jnp.tril(jnp.ones((span, span), dtype=bool))
    w = jnp.where(tri[None], w_sp[:, :span, :span], 0.0)
    vc = v.reshape(bsz, t_len // span, span, N_SGU_GROUPS, SGU_GROUP_DIM)
    mixed = jnp.einsum('gts,bcsgd->bctgd', w, vc) + b_sp[:, :span].T[None, None, :, :, None]
    o = u * mixed.reshape(bsz, t_len, SGU_WIDTH)
    return o, q_mem, v


def memory_kv(mem, g, w):
    kv = rmsnorm(mem, g) @ w
    k, v = jnp.split(kv, 2, axis=-1)
    return split_heads(k, N_MEM_HEADS), split_heads(v, N_MEM_HEADS)


def memory_attention(q_mem, mem_k, mem_v):
    q = split_heads(q_mem, N_MEM_HEADS)
    s = jnp.einsum('bqhd,bmhd->bhqm', q, mem_k, preferred_element_type=jnp.float32) * SB_SCALE
    p = jax.nn.softmax(s, axis=-1)
    o = jnp.einsum('bhqm,bmhd->bqhd', p.astype(mem_v.dtype), mem_v)
    return o.reshape(o.shape[:2] + (MEM_WIDTH,))


def swiglu(h, w_gate, w_up, w_down):
    return (jax.nn.silu(h @ w_gate) * (h @ w_up)) @ w_down


def setup_inputs(seed: int = 0) -> dict:
    key = jax.random.key(seed)
    ks = jax.random.split(key, 24)
    f32 = jnp.float32
    nrm = lambda k, shape, scale: jax.random.normal(k, shape, f32) * scale
    in_a = 3 * SB_WIDTH + MEM_WIDTH
    in_b = 2 * SGU_WIDTH + MEM_WIDTH
    mix_w = SB_WIDTH + MEM_WIDTH
    return {
        "x_prompt": nrm(ks[0], (BATCH, SEQ, D_MODEL), 1.0),
        "x_sample": nrm(ks[1], (DEC_BATCH, DEC_SEQ, D_MODEL), 1.0),
        "cache_sb_k": nrm(ks[2], (N_A_LAYERS, DEC_BATCH, PAST_LEN, N_SB_HEADS, HEAD_DIM), 1.0),
        "cache_sb_v": nrm(ks[3], (N_A_LAYERS, DEC_BATCH, PAST_LEN, N_SB_HEADS, HEAD_DIM), 1.0),
        "cache_mem_k": nrm(ks[4], (DEPTH, DEC_BATCH, N_MEM, N_MEM_HEADS, HEAD_DIM), 1.0),
        "cache_mem_v": nrm(ks[5], (DEPTH, DEC_BATCH, N_MEM, N_MEM_HEADS, HEAD_DIM), 1.0),
        "mem_prompt": nrm(ks[6], (BATCH, N_MEM, D_MODEL), 1.0),
        "g_mix": 1.0 + nrm(ks[7], (DEPTH, D_MODEL), 0.02),
        "w_in_a": nrm(ks[8], (N_A_LAYERS, D_MODEL, in_a), D_MODEL ** -0.5),
        "w_in_b": nrm(ks[9], (N_B_LAYERS, D_MODEL, in_b), D_MODEL ** -0.5),
        "w_sp": nrm(ks[10], (N_B_LAYERS, N_SGU_GROUPS, MLP_CHUNK, MLP_CHUNK), MLP_CHUNK ** -0.5),
        "b_sp": 1.0 + nrm(ks[11], (N_B_LAYERS, N_SGU_GROUPS, MLP_CHUNK), 0.02),
        "g_sgu": 1.0 + nrm(ks[12], (N_B_LAYERS, SGU_WIDTH), 0.02),
        "g_mem": 1.0 + nrm(ks[13], (DEPTH, D_MODEL), 0.02),
        "w_mem_kv": nrm(ks[14], (DEPTH, D_MODEL, 2 * MEM_WIDTH), D_MODEL ** -0.5),
        "w_out": nrm(ks[15], (DEPTH, mix_w, D_MODEL), mix_w ** -0.5),
        "g_ffn": 1.0 + nrm(ks[16], (DEPTH, D_MODEL), 0.02),
        "w_gate": nrm(ks[17], (DEPTH, D_MODEL, D_FF), D_MODEL ** -0.5),
        "w_up": nrm(ks[18], (DEPTH, D_MODEL, D_FF), D_MODEL ** -0.5),
        "w_down": nrm(ks[19], (DEPTH, D_FF, D_MODEL), D_FF ** -0.5),
        "g_final": 1.0 + nrm(ks[20], (D_MODEL,), 0.02),
    }


def reference(x_prompt, x_sample, cache_sb_k, cache_sb_v, cache_mem_k, cache_mem_v,
              mem_prompt, g_mix, w_in_a, w_in_b, w_sp, b_sp, g_sgu, g_mem, w_mem_kv,
              w_out, g_ffn, w_gate, w_up, w_down, g_final):
    y_p, y_s = x_prompt, x_sample
    sb_k_p, sb_v_p, sb_k_s, sb_v_s = [], [], [], []
    mem_k_p, mem_v_p, sgu_v_s = [], [], []
    for l in range(DEPTH):
        mk_p, mv_p = memory_kv(mem_prompt, g_mem[l], w_mem_kv[l])
        mem_k_p.append(mk_p)
        mem_v_p.append(mv_p)
        h_p = rmsnorm(y_p, g_mix[l])
        h_s = rmsnorm(y_s, g_mix[l])
        if l % 2 == 0:
            ia = l // 2
            o_p, qm_p, k_p, v_p = stick_breaking_mixer(h_p, w_in_a[ia])
            o_s, qm_s, k_s, v_s = stick_breaking_mixer(h_s, w_in_a[ia], cache_sb_k[ia], cache_sb_v[ia])
            sb_k_p.append(k_p)
            sb_v_p.append(v_p)
            sb_k_s.append(k_s)
            sb_v_s.append(v_s)
        else:
            ib = l // 2
            o_p, qm_p, _ = spatial_gating_mixer(h_p, w_in_b[ib], w_sp[ib], b_sp[ib], g_sgu[ib])
            o_s, qm_s, vrows_s = spatial_gating_mixer(h_s, w_in_b[ib], w_sp[ib], b_sp[ib], g_sgu[ib])
            sgu_v_s.append(vrows_s)
        m_p = memory_attention(qm_p, mk_p, mv_p)
        m_s = memory_attention(qm_s, cache_mem_k[l], cache_mem_v[l])
        y_p = y_p + jnp.concatenate([o_p, m_p], axis=-1) @ w_out[l]
        y_s = y_s + jnp.concatenate([o_s, m_s], axis=-1) @ w_out[l]
        y_p = y_p + swiglu(rmsnorm(y_p, g_ffn[l]), w_gate[l], w_up[l], w_down[l])
        y_s = y_s + swiglu(rmsnorm(y_s, g_ffn[l]), w_gate[l], w_up[l], w_down[l])
    y_prompt = rmsnorm(y_p, g_final)
    y_sample = rmsnorm(y_s, g_final)
    sb_k_prompt = jnp.stack(sb_k_p)
    sb_v_prompt = jnp.stack(sb_v_p)
    sb_k_sample = jnp.stack(sb_k_s)
    sb_v_sample = jnp.stack(sb_v_s)
    mem_k_prompt = jnp.stack(mem_k_p)
    mem_v_prompt = jnp.stack(mem_v_p)
    sgu_v_sample = jnp.stack(sgu_v_s)
    return (y_prompt, y_sample, sb_k_prompt, sb_v_prompt, sb_k_sample, sb_v_sample,
            mem_k_prompt, mem_v_prompt, sgu_v_sample)
```

```python
import functools

import jax
import jax.numpy as jnp
from jax import lax
from jax.experimental import pallas as pl
from jax.experimental.pallas import tpu as pltpu

F32 = jnp.float32
BF16 = jnp.bfloat16

D_MODEL = 1024
HEAD_DIM = 64
SB_WIDTH = 768
N_SB_HEADS = SB_WIDTH // HEAD_DIM
SGU_WIDTH = 768
MLP_CHUNK = 128
MEM_WIDTH = 256
N_MEM = 256
N_MEM_HEADS = MEM_WIDTH // HEAD_DIM
D_FF = 2816
EPS = 1e-6
SB_SCALE = HEAD_DIM ** -0.5

LANES = 128
V7X_VMEM_BYTES = 64 * 1024 * 1024
VMEM_LIMIT = V7X_VMEM_BYTES - 8 * 1024 * 1024

SB_DEAD_LOG = -106.0

SB_TQ = 256
SB_TK = 256
FF_CHUNK = 256


def _cparams(*semantics):
    return pltpu.CompilerParams(dimension_semantics=semantics,
                                vmem_limit_bytes=VMEM_LIMIT)


def _const_spec(shape):
    zeros = (0,) * len(shape)
    return pl.BlockSpec(shape, lambda *_: zeros, pipeline_mode=pl.Buffered(1))


def _rmsnorm_rows(x, g):
    ms = jnp.mean(x * x, axis=-1, keepdims=True)
    return (x * lax.rsqrt(ms + EPS)) * g


def _strict_upper(n):
    j = lax.broadcasted_iota(jnp.int32, (n, n), 0)
    s = lax.broadcasted_iota(jnp.int32, (n, n), 1)
    return jnp.where(j > s, 1.0, 0.0).astype(BF16)


def _norm_proj_kernel(x_ref, g_ref, w_ref, *rest, plan, has_gate_norm):
    if has_gate_norm:
        gn_ref, out_refs = rest[0], rest[1:]
    else:
        gn_ref, out_refs = None, rest
    h = _rmsnorm_rows(x_ref[...], g_ref[...]).astype(BF16)
    k = 0
    for lo, hi, kind in plan:
        p = jnp.dot(h, w_ref[:, lo:hi], preferred_element_type=F32)
        if kind == "gelu":
            p = jax.nn.gelu(p)
        elif kind == "gelu_norm":
            p = _rmsnorm_rows(jax.nn.gelu(p), gn_ref[...])
        for dt in _PLAN_DTYPES[kind]:
            out_refs[k][...] = p.astype(dt)
            k += 1


_PLAN_DTYPES = {
    "bf16": (BF16,),
    "f32": (F32,),
    "f32+bf16": (F32, BF16),
    "gelu": (F32,),
    "gelu_norm": (F32, BF16),
}


def _norm_proj(x, g, w, plan, *, tm, gate_norm=None, name):
    m, d = x.shape
    n = w.shape[1]
    out_shape, out_specs = [], []
    for lo, hi, kind in plan:
        for dt in _PLAN_DTYPES[kind]:
            out_shape.append(jax.ShapeDtypeStruct((m, hi - lo), dt))
            out_specs.append(pl.BlockSpec((tm, hi - lo), lambda i: (i, 0)))
    in_specs = [pl.BlockSpec((tm, d), lambda i: (i, 0)),
                _const_spec((1, d)), _const_spec((d, n))]
    args = [x, g.reshape(1, d), w.astype(BF16)]
    if gate_norm is not None:
        in_specs.append(_const_spec((1, gate_norm.shape[-1])))
        args.append(gate_norm.reshape(1, -1))
    return pl.pallas_call(
        functools.partial(_norm_proj_kernel, plan=plan,
                          has_gate_norm=gate_norm is not None),
        out_shape=out_shape, grid=(m // tm,), in_specs=in_specs,
        out_specs=out_specs, compiler_params=_cparams("arbitrary"), name=name,
    )(*args)


def _softplus(z):
    return jnp.maximum(z, 0.0) + jnp.log1p(jnp.exp(-jnp.abs(z)))


def _sb_tile(q_h, kb, vb_h, upper2, carry, causal):
    z = lax.dot_general(q_h, kb, (((1,), (1,)), ((), ())),
                        preferred_element_type=F32)
    log_keep = -_softplus(z)
    if causal is not None:
        log_keep = jnp.where(causal, log_keep, 0.0)
    hi = log_keep.astype(BF16)
    lo = (log_keep - hi.astype(F32)).astype(BF16)
    log_rest = jnp.dot(jnp.concatenate([hi, lo], axis=1), upper2,
                       preferred_element_type=F32) + carry
    a = jnp.exp(z + log_keep + log_rest)
    if causal is not None:
        a = jnp.where(causal, a, 0.0)
    pv = jnp.dot(a.astype(BF16), vb_h, preferred_element_type=F32)
    return pv, jnp.sum(log_keep, axis=-1, keepdims=True)


def _sb_prompt_kernel(q_ref, k_ref, v_ref, o_ref, *, tq, tk):
    i = pl.program_id(2)
    lane = lax.broadcasted_iota(jnp.int32, (1, LANES), 1)
    head0 = lane < HEAD_DIM
    q = q_ref[...] * jnp.asarray(SB_SCALE, BF16)
    zq = jnp.zeros_like(q)
    q_heads = (jnp.where(head0, q, zq), jnp.where(head0, zq, q))
    upper = _strict_upper(tk)
    upper2 = jnp.concatenate([upper, upper], axis=0)

    def sweep(j, carries, causal):
        start = pl.multiple_of(j * tk, tk)
        kb = k_ref[pl.ds(start, tk), :]
        vb = v_ref[pl.ds(start, tk), :]
        zv = jnp.zeros_like(vb)
        v_heads = (jnp.where(head0, vb, zv), jnp.where(head0, zv, vb))
        pv, new = 0.0, []
        for h in range(2):
            pv_h, s_h = _sb_tile(q_heads[h], kb, v_heads[h], upper2,
                                 carries[h], causal)
            pv = pv + pv_h
            new.append(carries[h] + s_h)
        return pv, tuple(new)

    row = lax.broadcasted_iota(jnp.int32, (tq, tk), 0)
    col = lax.broadcasted_iota(jnp.int32, (tq, tk), 1)
    zero_carry = jnp.zeros((tq, 1), F32)
    acc, carries = sweep(i, (zero_carry, zero_carry), col < row)

    def live(carries):
        return jnp.max(jnp.maximum(carries[0], carries[1]))

    def cond(state):
        j, _, _, top = state
        return jnp.logical_and(j >= 0, top > SB_DEAD_LOG)

    def body(state):
        j, acc, carries, _ = state
        pv, carries = sweep(j, carries, None)
        return j - 1, acc + pv, carries, live(carries)

    _, acc, _, _ = lax.while_loop(cond, body, (i - 1, acc, carries, live(carries)))
    o_ref[...] = acc.astype(o_ref.dtype)


def _sb_prompt(q, k, v, *, batch, seq):
    tq, tk = SB_TQ, SB_TK
    assert tq == tk and seq % tq == 0
    q3, k3, v3 = (a.reshape(batch, seq, SB_WIDTH) for a in (q, k, v))
    n_pairs = SB_WIDTH // LANES
    q_spec = pl.BlockSpec((None, tq, LANES), lambda b, p, i: (b, i, p))
    kv_spec = pl.BlockSpec((None, seq, LANES), lambda b, p, i: (b, 0, p))
    out = pl.pallas_call(
        functools.partial(_sb_prompt_kernel, tq=tq, tk=tk),
        out_shape=jax.ShapeDtypeStruct((batch, seq, SB_WIDTH), BF16),
        grid=(batch, n_pairs, seq // tq),
        in_specs=[q_spec, kv_spec, kv_spec], out_specs=q_spec,
        compiler_params=_cparams("arbitrary", "arbitrary", "arbitrary"),
        name="sb_prompt",
    )(q3, k3, v3)
    return out.reshape(batch * seq, SB_WIDTH)


def _sb_decode_kernel(q_ref, kn_ref, vn_ref, kc_ref, vc_ref, o_ref,
                      acc_ref, carry_ref, *, t_new, kv_block, tk):
    j = pl.program_id(1)
    n_rows = N_SB_HEADS * t_new
    row_head = lax.broadcasted_iota(jnp.int32, (n_rows, SB_WIDTH), 0) // t_new
    col_head = lax.broadcasted_iota(jnp.int32, (n_rows, SB_WIDTH), 1) // HEAD_DIM
    q_rows = jnp.concatenate([q_ref[...]] * N_SB_HEADS, axis=0)
    q_bd = jnp.where(row_head == col_head, q_rows * SB_SCALE, 0.0).astype(BF16)

    def tile(kb, vb, n_keys, causal):
        upper = _strict_upper(n_keys)
        upper2 = jnp.concatenate([upper, upper], axis=0)
        pv, s = _sb_tile(q_bd, kb, vb, upper2, carry_ref[...], causal)
        acc_ref[...] += pv
        carry_ref[...] += s

    @pl.when(j == 0)
    def _():
        acc_ref[...] = jnp.zeros_like(acc_ref)
        carry_ref[...] = jnp.zeros_like(carry_ref)
        n_pad = kn_ref.shape[0]
        q_idx = lax.broadcasted_iota(jnp.int32, (n_rows, n_pad), 0) % t_new
        s_idx = lax.broadcasted_iota(jnp.int32, (n_rows, n_pad), 1)
        tile(kn_ref[...], vn_ref[...], n_pad, s_idx < q_idx)

    for c in reversed(range(kv_block // tk)):
        @pl.when(jnp.max(carry_ref[...]) > SB_DEAD_LOG)
        def _():
            tile(kc_ref[c * tk:(c + 1) * tk, :].astype(BF16),
                 vc_ref[c * tk:(c + 1) * tk, :].astype(BF16), tk, None)

    @pl.when(j == pl.num_programs(1) - 1)
    def _():
        out_head = lax.broadcasted_iota(jnp.int32, (t_new, SB_WIDTH), 1) // HEAD_DIM
        out = jnp.zeros((t_new, SB_WIDTH), F32)
        for h in range(N_SB_HEADS):
            out = out + jnp.where(out_head == h,
                                  acc_ref[h * t_new:(h + 1) * t_new, :], 0.0)
        o_ref[...] = out.astype(o_ref.dtype)


def _sb_decode(q, k_new, v_new, cache_k, cache_v, *, batch, t_new):
    past = cache_k.shape[1]
    kv_block, tk = 1024, 256
    assert past % kv_block == 0
    n_blocks = past // kv_block
    pad = ((0, 0), (0, LANES - t_new), (0, 0))
    kn = jnp.pad(k_new.reshape(batch, t_new, SB_WIDTH), pad)
    vn = jnp.pad(v_new.reshape(batch, t_new, SB_WIDTH), pad)
    row_spec = pl.BlockSpec((None, t_new, SB_WIDTH), lambda b, j: (b, 0, 0))
    new_spec = pl.BlockSpec((None, LANES, SB_WIDTH), lambda b, j: (b, 0, 0))
    cache_spec = pl.BlockSpec((None, kv_block, SB_WIDTH),
                              lambda b, j: (b, n_blocks - 1 - j, 0))
    n_rows = N_SB_HEADS * t_new
    out = pl.pallas_call(
        functools.partial(_sb_decode_kernel, t_new=t_new, kv_block=kv_block, tk=tk),
        out_shape=jax.ShapeDtypeStruct((batch, t_new, SB_WIDTH), BF16),
        grid=(batch, n_blocks),
        in_specs=[row_spec, new_spec, new_spec, cache_spec, cache_spec],
        out_specs=row_spec,
        scratch_shapes=[pltpu.VMEM((n_rows, SB_WIDTH), F32),
                        pltpu.VMEM((n_rows, 1), F32)],
        compiler_params=_cparams("arbitrary", "arbitrary"),
        name="sb_decode",
    )(q.reshape(batch, t_new, SB_WIDTH), kn, vn, cache_k, cache_v)
    return out.reshape(batch * t_new, SB_WIDTH)


def _sgu_kernel(u_ref, v_ref, w_ref, b_ref, o_ref, *, span, n_chunks):
    lane = lax.broadcasted_iota(jnp.int32, (1, LANES), 1)
    group0 = lane < HEAD_DIM
    t = lax.broadcasted_iota(jnp.int32, (span, span), 0)
    s = lax.broadcasted_iota(jnp.int32, (span, span), 1)
    tril = s <= t
    for p in range(SGU_WIDTH // LANES):
        w_pair = jnp.concatenate(
            [jnp.where(tril, w_ref[2 * p], 0.0), jnp.where(tril, w_ref[2 * p + 1], 0.0)],
            axis=1).astype(BF16)
        cols = slice(p * LANES, (p + 1) * LANES)
        for c in range(n_chunks):
            rows = slice(c * span, (c + 1) * span)
            vv = v_ref[rows, cols]
            zv = jnp.zeros_like(vv)
            v_pair = jnp.concatenate([jnp.where(group0, vv, zv),
                                      jnp.where(group0, zv, vv)], axis=0)
            mixed = jnp.dot(w_pair, v_pair, preferred_element_type=F32) + b_ref[:, cols]
            o_ref[rows, cols] = (u_ref[rows, cols] * mixed).astype(o_ref.dtype)


def _sgu(u, v, w, bias, *, tm, name):
    m = u.shape[0]
    span = w.shape[-1]
    blk = pl.BlockSpec((tm, SGU_WIDTH), lambda i: (i, 0))
    return pl.pallas_call(
        functools.partial(_sgu_kernel, span=span, n_chunks=tm // span),
        out_shape=jax.ShapeDtypeStruct((m, SGU_WIDTH), BF16),
        grid=(m // tm,),
        in_specs=[blk, blk, _const_spec(w.shape), _const_spec(bias.shape)],
        out_specs=blk, compiler_params=_cparams("arbitrary"), name=name,
    )(u, v, w, bias)


def _mix_out_kernel(y_ref, o_ref, qm_ref, mk_ref, mv_ref, w_ref, out_ref):
    lane = lax.broadcasted_iota(jnp.int32, (1, MEM_WIDTH), 1) // HEAD_DIM
    qm = qm_ref[...] * jnp.asarray(SB_SCALE, BF16)
    zq = jnp.zeros_like(qm)
    mk = mk_ref[...]
    mv = mv_ref[...]
    zv = jnp.zeros_like(mv)
    mem = 0.0
    for h in range(N_MEM_HEADS):
        s = lax.dot_general(jnp.where(lane == h, qm, zq), mk,
                            (((1,), (1,)), ((), ())), preferred_element_type=F32)
        e = jnp.exp(s - jnp.max(s, axis=-1, keepdims=True))
        p = e / jnp.sum(e, axis=-1, keepdims=True)
        mem = mem + jnp.dot(p.astype(BF16), jnp.where(lane == h, mv, zv),
                            preferred_element_type=F32)
    mixed = jnp.concatenate([o_ref[...], mem.astype(BF16)], axis=1)
    out_ref[...] = y_ref[...] + jnp.dot(mixed, w_ref[...], preferred_element_type=F32)


def _mix_out(y, o, qm, mem_k, mem_v, w_out, *, tm, rows_per_batch, name):
    m = y.shape[0]
    assert rows_per_batch % tm == 0
    per = rows_per_batch // tm
    mem_spec = pl.BlockSpec((None, N_MEM, MEM_WIDTH), lambda i: (i // per, 0, 0))
    return pl.pallas_call(
        _mix_out_kernel,
        out_shape=jax.ShapeDtypeStruct((m, D_MODEL), F32),
        grid=(m // tm,),
        in_specs=[pl.BlockSpec((tm, D_MODEL), lambda i: (i, 0)),
                  pl.BlockSpec((tm, SB_WIDTH), lambda i: (i, 0)),
                  pl.BlockSpec((tm, MEM_WIDTH), lambda i: (i, 0)),
                  mem_spec, mem_spec, _const_spec((D_MODEL, D_MODEL))],
        out_specs=pl.BlockSpec((tm, D_MODEL), lambda i: (i, 0)),
        compiler_params=_cparams("arbitrary"), name=name,
    )(y, o, qm, mem_k, mem_v, w_out.astype(BF16))


def _ffn_kernel(y_ref, g_ref, wg_ref, wu_ref, wd_ref, *rest, final):
    if final:
        gf_ref, out_ref = rest
    else:
        (out_ref,) = rest
    y = y_ref[...]
    h = _rmsnorm_rows(y, g_ref[...]).astype(BF16)
    acc = y
    for c in range(D_FF // FF_CHUNK):
        cols = slice(c * FF_CHUNK, (c + 1) * FF_CHUNK)
        gate = jnp.dot(h, wg_ref[:, cols], preferred_element_type=F32)
        up = jnp.dot(h, wu_ref[:, cols], preferred_element_type=F32)
        act = (jax.nn.silu(gate) * up).astype(BF16)
        acc = acc + jnp.dot(act, wd_ref[cols, :], preferred_element_type=F32)
    if final:
        acc = _rmsnorm_rows(acc, gf_ref[...])
    out_ref[...] = acc


def _ffn(y, g, w_gate, w_up, w_down, *, tm, g_final=None, name):
    m = y.shape[0]
    row = pl.BlockSpec((tm, D_MODEL), lambda i: (i, 0))
    in_specs = [row, _const_spec((1, D_MODEL)), _const_spec((D_MODEL, D_FF)),
                _const_spec((D_MODEL, D_FF)), _const_spec((D_FF, D_MODEL))]
    args = [y, g.reshape(1, D_MODEL), w_gate.astype(BF16), w_up.astype(BF16),
            w_down.astype(BF16)]
    if g_final is not None:
        in_specs.append(_const_spec((1, D_MODEL)))
        args.append(g_final.reshape(1, D_MODEL))
    return pl.pallas_call(
        functools.partial(_ffn_kernel, final=g_final is not None),
        out_shape=jax.ShapeDtypeStruct((m, D_MODEL), F32),
        grid=(m // tm,), in_specs=in_specs, out_specs=row,
        compiler_params=_cparams("arbitrary"), name=name,
    )(*args)


_PLAN_A_PROMPT = ((0, SB_WIDTH, "bf16"), (SB_WIDTH, 2 * SB_WIDTH, "f32+bf16"),
                  (2 * SB_WIDTH, 3 * SB_WIDTH, "f32+bf16"),
                  (3 * SB_WIDTH, 3 * SB_WIDTH + MEM_WIDTH, "bf16"))
_PLAN_A_SAMPLE = ((0, SB_WIDTH, "f32"),) + _PLAN_A_PROMPT[1:]
_PLAN_B = ((0, SGU_WIDTH, "gelu"), (SGU_WIDTH, 2 * SGU_WIDTH, "gelu_norm"),
           (2 * SGU_WIDTH, 2 * SGU_WIDTH + MEM_WIDTH, "bf16"))
_PLAN_MEM = ((0, MEM_WIDTH, "f32+bf16"), (MEM_WIDTH, 2 * MEM_WIDTH, "f32+bf16"))


def kernel(x_prompt, x_sample, cache_sb_k, cache_sb_v, cache_mem_k, cache_mem_v,
           mem_prompt, g_mix, w_in_a, w_in_b, w_sp, b_sp, g_sgu, g_mem, w_mem_kv,
           w_out, g_ffn, w_gate, w_up, w_down, g_final):
    batch, seq, _ = x_prompt.shape
    dec_batch, dec_seq, _ = x_sample.shape
    depth = g_mix.shape[0]
    tm_p = 512
    tm_s = dec_batch * dec_seq

    y_p = x_prompt.reshape(batch * seq, D_MODEL)
    y_s = x_sample.reshape(tm_s, D_MODEL)
    mem_rows = mem_prompt.reshape(batch * N_MEM, D_MODEL)

    sb_k_p, sb_v_p, sb_k_s, sb_v_s, mem_k_p, mem_v_p, sgu_v_s = ([] for _ in range(7))
    for l in range(depth):
        mk_f, mk_b, mv_f, mv_b = _norm_proj(
            mem_rows, g_mem[l], w_mem_kv[l], _PLAN_MEM, tm=tm_p, name=f"mem_kv{l}")
        mem_k_p.append(mk_f.reshape(batch, N_MEM, N_MEM_HEADS, HEAD_DIM))
        mem_v_p.append(mv_f.reshape(batch, N_MEM, N_MEM_HEADS, HEAD_DIM))
        mk_b = mk_b.reshape(batch, N_MEM, MEM_WIDTH)
        mv_b = mv_b.reshape(batch, N_MEM, MEM_WIDTH)
        cmk = cache_mem_k[l].reshape(dec_batch, N_MEM, MEM_WIDTH).astype(BF16)
        cmv = cache_mem_v[l].reshape(dec_batch, N_MEM, MEM_WIDTH).astype(BF16)

        if l % 2 == 0:
            ia = l // 2
            q, k_f, k_b, v_f, v_b, qm_p = _norm_proj(
                y_p, g_mix[l], w_in_a[ia], _PLAN_A_PROMPT, tm=tm_p, name=f"in_a{l}_p")
            sb_k_p.append(k_f.reshape(batch, seq, N_SB_HEADS, HEAD_DIM))
            sb_v_p.append(v_f.reshape(batch, seq, N_SB_HEADS, HEAD_DIM))
            o_p = _sb_prompt(q, k_b, v_b, batch=batch, seq=seq)

            q, k_f, k_b, v_f, v_b, qm_s = _norm_proj(
                y_s, g_mix[l], w_in_a[ia], _PLAN_A_SAMPLE, tm=tm_s, name=f"in_a{l}_s")
            sb_k_s.append(k_f.reshape(dec_batch, dec_seq, N_SB_HEADS, HEAD_DIM))
            sb_v_s.append(v_f.reshape(dec_batch, dec_seq, N_SB_HEADS, HEAD_DIM))
            past = cache_sb_k.shape[2]
            o_s = _sb_decode(q, k_b, v_b,
                             cache_sb_k[ia].reshape(dec_batch, past, SB_WIDTH),
                             cache_sb_v[ia].reshape(dec_batch, past, SB_WIDTH),
                             batch=dec_batch, t_new=dec_seq)
        else:
            ib = l // 2
            u, _, v_b, qm_p = _norm_proj(
                y_p, g_mix[l], w_in_b[ib], _PLAN_B, tm=tm_p, gate_norm=g_sgu[ib],
                name=f"in_b{l}_p")
            span = min(seq, MLP_CHUNK)
            bias = jnp.repeat(b_sp[ib][:, :span].T, HEAD_DIM, axis=1)
            o_p = _sgu(u, v_b, w_sp[ib][:, :span, :span], bias, tm=tm_p, name=f"sgu{l}_p")

            u, v_f, v_b, qm_s = _norm_proj(
                y_s, g_mix[l], w_in_b[ib], _PLAN_B, tm=tm_s, gate_norm=g_sgu[ib],
                name=f"in_b{l}_s")
            sgu_v_s.append(v_f.reshape(dec_batch, dec_seq, SGU_WIDTH))
            span = min(dec_seq, MLP_CHUNK)
            reps = MLP_CHUNK // span
            eye = jnp.eye(reps, dtype=F32)
            w_bd = jnp.einsum("ab,gts->gatbs", eye, w_sp[ib][:, :span, :span]).reshape(
                -1, MLP_CHUNK, MLP_CHUNK)
            bias = jnp.tile(jnp.repeat(b_sp[ib][:, :span].T, HEAD_DIM, axis=1), (reps, 1))
            o_s = _sgu(u, v_b, w_bd, bias, tm=tm_s, name=f"sgu{l}_s")

        y_p = _mix_out(y_p, o_p, qm_p, mk_b, mv_b, w_out[l], tm=tm_p,
                       rows_per_batch=seq, name=f"mix_out{l}_p")
        y_s = _mix_out(y_s, o_s, qm_s, cmk, cmv, w_out[l], tm=dec_seq,
                       rows_per_batch=dec_seq, name=f"mix_out{l}_s")
        last = g_final if l == depth - 1 else None
        y_p = _ffn(y_p, g_ffn[l], w_gate[l], w_up[l], w_down[l], tm=tm_p,
                   g_final=last, name=f"ffn{l}_p")
        y_s = _ffn(y_s, g_ffn[l], w_gate[l], w_up[l], w_down[l], tm=tm_s,
                   g_final=last, name=f"ffn{l}_s")

    return (y_p.reshape(batch, seq, D_MODEL), y_s.reshape(dec_batch, dec_seq, D_MODEL),
            jnp.stack(sb_k_p), jnp.stack(sb_v_p), jnp.stack(sb_k_s), jnp.stack(sb_v_s),
            jnp.stack(mem_k_p), jnp.stack(mem_v_p), jnp.stack(sgu_v_s))
```

```python
import functools

import jax
import jax.numpy as jnp
from jax import lax
from jax.experimental import pallas as pl
from jax.experimental.pallas import tpu as pltpu

F32 = jnp.float32
BF16 = jnp.bfloat16

D_MODEL = 1024
HEAD_DIM = 64
SB_WIDTH = 768
N_SB_HEADS = SB_WIDTH // HEAD_DIM
SGU_WIDTH = 768
MLP_CHUNK = 128
MEM_WIDTH = 256
N_MEM = 256
N_MEM_HEADS = MEM_WIDTH // HEAD_DIM
D_FF = 2816
EPS = 1e-6
SB_SCALE = HEAD_DIM ** -0.5
LOG2E = 1.4426950408889634

LANES = 128
V7X_VMEM_BYTES = 64 * 1024 * 1024
VMEM_LIMIT = V7X_VMEM_BYTES - 8 * 1024 * 1024

SB_DEAD_MASS = 106.0

SB_TQ = 256
SB_TK = 256
FF_CHUNK = 256


def _cparams(*semantics):
    return pltpu.CompilerParams(dimension_semantics=semantics,
                                vmem_limit_bytes=VMEM_LIMIT)


def _const_spec(shape):
    zeros = (0,) * len(shape)
    return pl.BlockSpec(shape, lambda *_: zeros, pipeline_mode=pl.Buffered(1))


def _rmsnorm_rows(x, g):
    ms = jnp.mean(x * x, axis=-1, keepdims=True)
    return (x * lax.rsqrt(ms + EPS)) * g


def _dot_nt(a, b):
    return lax.dot_general(a, b, (((1,), (1,)), ((), ())), preferred_element_type=F32)


_PLAN_FORMS = {
    "bf16": ("bf16",),
    "f32": ("f32",),
    "f32+bf16": ("f32", "bf16"),
    "f32T+bf16": ("f32T", "bf16"),
    "gelu": ("f32",),
    "gelu_norm": ("f32", "bf16"),
}


def _norm_proj_kernel(x_ref, g_ref, w_ref, *rest, plan, has_gate_norm):
    if has_gate_norm:
        gn_ref, out_refs = rest[0], rest[1:]
    else:
        gn_ref, out_refs = None, rest
    h = _rmsnorm_rows(x_ref[...], g_ref[...]).astype(BF16)
    k = 0
    for lo, hi, kind in plan:
        p = jnp.dot(h, w_ref[:, lo:hi], preferred_element_type=F32)
        if kind == "gelu":
            p = jax.nn.gelu(p)
        elif kind == "gelu_norm":
            p = _rmsnorm_rows(jax.nn.gelu(p), gn_ref[...])
        for form in _PLAN_FORMS[kind]:
            if form == "f32T":
                out_refs[k][...] = p.T
            else:
                out_refs[k][...] = p.astype(BF16 if form == "bf16" else F32)
            k += 1


def _norm_proj(x, g, w, plan, *, tm, rows_per_batch=None, gate_norm=None, name):
    m, d = x.shape
    n = w.shape[1]
    out_shape, out_specs = [], []
    for lo, hi, kind in plan:
        for form in _PLAN_FORMS[kind]:
            if form == "f32T":
                per = rows_per_batch // tm
                out_shape.append(jax.ShapeDtypeStruct(
                    (m // rows_per_batch, hi - lo, rows_per_batch), F32))
                out_specs.append(pl.BlockSpec(
                    (None, hi - lo, tm), lambda i, per=per: (i // per, 0, i % per)))
            else:
                out_shape.append(jax.ShapeDtypeStruct(
                    (m, hi - lo), BF16 if form == "bf16" else F32))
                out_specs.append(pl.BlockSpec((tm, hi - lo), lambda i: (i, 0)))
    in_specs = [pl.BlockSpec((tm, d), lambda i: (i, 0)),
                _const_spec((1, d)), _const_spec((d, n))]
    args = [x, g.reshape(1, d), w.astype(BF16)]
    if gate_norm is not None:
        in_specs.append(_const_spec((1, gate_norm.shape[-1])))
        args.append(gate_norm.reshape(1, -1))
    return pl.pallas_call(
        functools.partial(_norm_proj_kernel, plan=plan,
                          has_gate_norm=gate_norm is not None),
        out_shape=out_shape, grid=(m // tm,), in_specs=in_specs,
        out_specs=out_specs, compiler_params=_cparams("arbitrary"), name=name,
    )(*args)


def _strict_upper2(n):
    j = lax.broadcasted_iota(jnp.int32, (2 * n, n), 0) % n
    s = lax.broadcasted_iota(jnp.int32, (2 * n, n), 1)
    return jnp.where(j > s, 1.0, 0.0).astype(BF16)


def _sb_weights(z, upper2, mass, causal):
    top = jnp.maximum(z, 0.0)
    soft = top + jnp.log(1.0 + jnp.exp2(jnp.abs(z) * -LOG2E))
    log_beta = z - soft
    if causal is not None:
        soft = jnp.where(causal, soft, 0.0)
    hi = soft.astype(BF16)
    lo = (soft - hi.astype(F32)).astype(BF16)
    later = jnp.dot(jnp.concatenate([hi, lo], axis=1), upper2,
                    preferred_element_type=F32) + mass
    a = jnp.exp2((log_beta - later) * LOG2E)
    if causal is not None:
        a = jnp.where(causal, a, 0.0)
    return a.astype(BF16), mass + jnp.sum(soft, axis=-1, keepdims=True)


def _sb_prompt_kernel(q_ref, k_ref, v_ref, o_ref, *, tq, tk):
    i = pl.program_id(2)
    lane = lax.broadcasted_iota(jnp.int32, (1, LANES), 1)
    head0 = lane < HEAD_DIM
    q = q_ref[...] * jnp.asarray(SB_SCALE, BF16)
    zq = jnp.zeros_like(q)
    q2 = jnp.concatenate([jnp.where(head0, q, zq), jnp.where(head0, zq, q)], axis=0)
    upper2 = _strict_upper2(tk)

    def logits(j):
        return _dot_nt(q2, k_ref[pl.ds(pl.multiple_of(j * tk, tk), tk), :])

    def values(a, j):
        vb = v_ref[pl.ds(pl.multiple_of(j * tk, tk), tk), :]
        zv = jnp.zeros_like(vb)
        return (jnp.dot(a[:tq], jnp.where(head0, vb, zv), preferred_element_type=F32)
                + jnp.dot(a[tq:], jnp.where(head0, zv, vb), preferred_element_type=F32))

    row = lax.broadcasted_iota(jnp.int32, (2 * tq, tk), 0) % tq
    col = lax.broadcasted_iota(jnp.int32, (2 * tq, tk), 1)
    has_prev = i > 0
    j_prev = jnp.maximum(i - 1, 0)
    z_diag, z_prev = logits(i), logits(j_prev)
    a_diag, mass_diag = _sb_weights(z_diag, upper2, jnp.zeros((2 * tq, 1), F32),
                                    col < row)
    a_prev, mass_prev = _sb_weights(z_prev, upper2, mass_diag, None)
    acc = values(a_diag, i) + jnp.where(has_prev, values(a_prev, j_prev), 0.0)
    mass = jnp.where(has_prev, mass_prev, mass_diag)

    def cond(state):
        j, _, _, least = state
        return jnp.logical_and(j >= 0, least < SB_DEAD_MASS)

    def body(state):
        j, acc, mass, _ = state
        a, mass = _sb_weights(logits(j), upper2, mass, None)
        return j - 1, acc + values(a, j), mass, jnp.min(mass)

    _, acc, _, _ = lax.while_loop(cond, body, (i - 2, acc, mass, jnp.min(mass)))
    o_ref[...] = acc.astype(o_ref.dtype)


def _sb_prompt(q, k, v, *, batch, seq):
    tq, tk = SB_TQ, SB_TK
    assert tq == tk and seq % tq == 0
    q3, k3, v3 = (a.reshape(batch, seq, SB_WIDTH) for a in (q, k, v))
    n_pairs = SB_WIDTH // LANES
    q_spec = pl.BlockSpec((None, tq, LANES), lambda b, p, i: (b, i, p))
    kv_spec = pl.BlockSpec((None, seq, LANES), lambda b, p, i: (b, 0, p))
    out = pl.pallas_call(
        functools.partial(_sb_prompt_kernel, tq=tq, tk=tk),
        out_shape=jax.ShapeDtypeStruct((batch, seq, SB_WIDTH), BF16),
        grid=(batch, n_pairs, seq // tq),
        in_specs=[q_spec, kv_spec, kv_spec], out_specs=q_spec,
        compiler_params=_cparams("arbitrary", "arbitrary", "arbitrary"),
        name="sb_prompt",
    )(q3, k3, v3)
    return out.reshape(batch * seq, SB_WIDTH)


def _sb_decode_kernel(q_ref, kn_ref, vn_ref, kc_ref, vc_ref, o_ref,
                      acc_ref, mass_ref, *, t_new, kv_block, tk):
    j = pl.program_id(1)
    n_rows = N_SB_HEADS * t_new
    row_head = lax.broadcasted_iota(jnp.int32, (n_rows, SB_WIDTH), 0) // t_new
    col_head = lax.broadcasted_iota(jnp.int32, (n_rows, SB_WIDTH), 1) // HEAD_DIM
    q_rows = jnp.concatenate([q_ref[...]] * N_SB_HEADS, axis=0)
    q_bd = jnp.where(row_head == col_head, q_rows * SB_SCALE, 0.0).astype(BF16)

    @pl.when(j == 0)
    def _():
        n_pad = kn_ref.shape[0]
        q_idx = lax.broadcasted_iota(jnp.int32, (n_rows, n_pad), 0) % t_new
        s_idx = lax.broadcasted_iota(jnp.int32, (n_rows, n_pad), 1)
        a, mass = _sb_weights(_dot_nt(q_bd, kn_ref[...]), _strict_upper2(n_pad),
                              jnp.zeros((n_rows, 1), F32), s_idx < q_idx)
        acc_ref[...] = jnp.dot(a, vn_ref[...], preferred_element_type=F32)
        mass_ref[...] = mass

    for c in reversed(range(kv_block // tk)):
        @pl.when(jnp.min(mass_ref[...]) < SB_DEAD_MASS)
        def _():
            keys = slice(c * tk, (c + 1) * tk)
            z = jnp.dot(q_bd, kc_ref[:, keys].astype(BF16), preferred_element_type=F32)
            a, mass = _sb_weights(z, _strict_upper2(tk), mass_ref[...], None)
            acc_ref[...] += _dot_nt(a, vc_ref[:, keys].astype(BF16))
            mass_ref[...] = mass

    @pl.when(j == pl.num_programs(1) - 1)
    def _():
        out_head = lax.broadcasted_iota(jnp.int32, (t_new, SB_WIDTH), 1) // HEAD_DIM
        out = jnp.zeros((t_new, SB_WIDTH), F32)
        for h in range(N_SB_HEADS):
            out = out + jnp.where(out_head == h,
                                  acc_ref[h * t_new:(h + 1) * t_new, :], 0.0)
        o_ref[...] = out.astype(o_ref.dtype)


def _sb_decode(q, k_new, v_new, cache_kt, cache_vt, *, batch, t_new):
    past = cache_kt.shape[2]
    kv_block, tk = 1024, 256
    assert past % kv_block == 0
    n_blocks = past // kv_block
    pad = ((0, 0), (0, LANES - t_new), (0, 0))
    kn = jnp.pad(k_new.reshape(batch, t_new, SB_WIDTH), pad)
    vn = jnp.pad(v_new.reshape(batch, t_new, SB_WIDTH), pad)
    row_spec = pl.BlockSpec((None, t_new, SB_WIDTH), lambda b, j: (b, 0, 0))
    new_spec = pl.BlockSpec((None, LANES, SB_WIDTH), lambda b, j: (b, 0, 0))
    cache_spec = pl.BlockSpec((None, SB_WIDTH, kv_block),
                              lambda b, j: (b, 0, n_blocks - 1 - j))
    n_rows = N_SB_HEADS * t_new
    out = pl.pallas_call(
        functools.partial(_sb_decode_kernel, t_new=t_new, kv_block=kv_block, tk=tk),
        out_shape=jax.ShapeDtypeStruct((batch, t_new, SB_WIDTH), BF16),
        grid=(batch, n_blocks),
        in_specs=[row_spec, new_spec, new_spec, cache_spec, cache_spec],
        out_specs=row_spec,
        scratch_shapes=[pltpu.VMEM((n_rows, SB_WIDTH), F32),
                        pltpu.VMEM((n_rows, 1), F32)],
        compiler_params=_cparams("arbitrary", "arbitrary"),
        name="sb_decode",
    )(q.reshape(batch, t_new, SB_WIDTH), kn, vn, cache_kt, cache_vt)
    return out.reshape(batch * t_new, SB_WIDTH)


def _heads_major(x):
    b, t, h, d = x.shape
    return jnp.transpose(x, (0, 2, 3, 1)).reshape(b, h * d, t)


def _heads_minor(xt, heads):
    b, hd, t = xt.shape
    return jnp.transpose(xt.reshape(b, heads, hd // heads, t), (0, 3, 1, 2))


def _sgu_kernel(u_ref, v_ref, w_ref, b_ref, o_ref, *, span, n_chunks):
    lane = lax.broadcasted_iota(jnp.int32, (1, LANES), 1)
    group0 = lane < HEAD_DIM
    t = lax.broadcasted_iota(jnp.int32, (span, span), 0)
    s = lax.broadcasted_iota(jnp.int32, (span, span), 1)
    tril = s <= t
    for p in range(SGU_WIDTH // LANES):
        w_pair = jnp.concatenate(
            [jnp.where(tril, w_ref[2 * p], 0.0), jnp.where(tril, w_ref[2 * p + 1], 0.0)],
            axis=1).astype(BF16)
        cols = slice(p * LANES, (p + 1) * LANES)
        for c in range(n_chunks):
            rows = slice(c * span, (c + 1) * span)
            vv = v_ref[rows, cols]
            zv = jnp.zeros_like(vv)
            v_pair = jnp.concatenate([jnp.where(group0, vv, zv),
                                      jnp.where(group0, zv, vv)], axis=0)
            mixed = jnp.dot(w_pair, v_pair, preferred_element_type=F32) + b_ref[:, cols]
            o_ref[rows, cols] = (u_ref[rows, cols] * mixed).astype(o_ref.dtype)


def _sgu(u, v, w, bias, *, tm, name):
    m = u.shape[0]
    span = w.shape[-1]
    blk = pl.BlockSpec((tm, SGU_WIDTH), lambda i: (i, 0))
    return pl.pallas_call(
        functools.partial(_sgu_kernel, span=span, n_chunks=tm // span),
        out_shape=jax.ShapeDtypeStruct((m, SGU_WIDTH), BF16),
        grid=(m // tm,),
        in_specs=[blk, blk, _const_spec(w.shape), _const_spec(bias.shape)],
        out_specs=blk, compiler_params=_cparams("arbitrary"), name=name,
    )(u, v, w, bias)


def _mix_out_kernel(y_ref, o_ref, qm_ref, mk_ref, mv_ref, w_ref, out_ref):
    lane = lax.broadcasted_iota(jnp.int32, (1, MEM_WIDTH), 1) // HEAD_DIM
    qm = qm_ref[...] * jnp.asarray(SB_SCALE, BF16)
    zq = jnp.zeros_like(qm)
    mk = mk_ref[...]
    mv = mv_ref[...]
    zv = jnp.zeros_like(mv)
    mem = 0.0
    for h in range(N_MEM_HEADS):
        s = _dot_nt(jnp.where(lane == h, qm, zq), mk)
        e = jnp.exp(s - jnp.max(s, axis=-1, keepdims=True))
        p = e / jnp.sum(e, axis=-1, keepdims=True)
        mem = mem + jnp.dot(p.astype(BF16), jnp.where(lane == h, mv, zv),
                            preferred_element_type=F32)
    mixed = jnp.concatenate([o_ref[...], mem.astype(BF16)], axis=1)
    out_ref[...] = y_ref[...] + jnp.dot(mixed, w_ref[...], preferred_element_type=F32)


def _mix_out(y, o, qm, mem_k, mem_v, w_out, *, tm, rows_per_batch, name):
    m = y.shape[0]
    assert rows_per_batch % tm == 0
    per = rows_per_batch // tm
    mem_spec = pl.BlockSpec((None, N_MEM, MEM_WIDTH), lambda i: (i // per, 0, 0))
    return pl.pallas_call(
        _mix_out_kernel,
        out_shape=jax.ShapeDtypeStruct((m, D_MODEL), F32),
        grid=(m // tm,),
        in_specs=[pl.BlockSpec((tm, D_MODEL), lambda i: (i, 0)),
                  pl.BlockSpec((tm, SB_WIDTH), lambda i: (i, 0)),
                  pl.BlockSpec((tm, MEM_WIDTH), lambda i: (i, 0)),
                  mem_spec, mem_spec, _const_spec((D_MODEL, D_MODEL))],
        out_specs=pl.BlockSpec((tm, D_MODEL), lambda i: (i, 0)),
        compiler_params=_cparams("arbitrary"), name=name,
    )(y, o, qm, mem_k, mem_v, w_out.astype(BF16))


def _ffn_kernel(y_ref, g_ref, wg_ref, wu_ref, wd_ref, *rest, final):
    if final:
        gf_ref, out_ref = rest
    else:
        (out_ref,) = rest
    y = y_ref[...]
    h = _rmsnorm_rows(y, g_ref[...]).astype(BF16)
    acc = y
    for c in range(D_FF // FF_CHUNK):
        cols = slice(c * FF_CHUNK, (c + 1) * FF_CHUNK)
        gate = jnp.dot(h, wg_ref[:, cols], preferred_element_type=F32)
        up = jnp.dot(h, wu_ref[:, cols], preferred_element_type=F32)
        act = (jax.nn.silu(gate) * up).astype(BF16)
        acc = acc + jnp.dot(act, wd_ref[cols, :], preferred_element_type=F32)
    if final:
        acc = _rmsnorm_rows(acc, gf_ref[...])
    out_ref[...] = acc


def _ffn(y, g, w_gate, w_up, w_down, *, tm, g_final=None, name):
    m = y.shape[0]
    row = pl.BlockSpec((tm, D_MODEL), lambda i: (i, 0))
    in_specs = [row, _const_spec((1, D_MODEL)), _const_spec((D_MODEL, D_FF)),
                _const_spec((D_MODEL, D_FF)), _const_spec((D_FF, D_MODEL))]
    args = [y, g.reshape(1, D_MODEL), w_gate.astype(BF16), w_up.astype(BF16),
            w_down.astype(BF16)]
    if g_final is not None:
        in_specs.append(_const_spec((1, D_MODEL)))
        args.append(g_final.reshape(1, D_MODEL))
    return pl.pallas_call(
        functools.partial(_ffn_kernel, final=g_final is not None),
        out_shape=jax.ShapeDtypeStruct((m, D_MODEL), F32),
        grid=(m // tm,), in_specs=in_specs, out_specs=row,
        compiler_params=_cparams("arbitrary"), name=name,
    )(*args)


_PLAN_A_PROMPT = ((0, SB_WIDTH, "bf16"), (SB_WIDTH, 2 * SB_WIDTH, "f32T+bf16"),
                  (2 * SB_WIDTH, 3 * SB_WIDTH, "f32T+bf16"),
                  (3 * SB_WIDTH, 3 * SB_WIDTH + MEM_WIDTH, "bf16"))
_PLAN_A_SAMPLE = ((0, SB_WIDTH, "f32"), (SB_WIDTH, 2 * SB_WIDTH, "f32+bf16"),
                  (2 * SB_WIDTH, 3 * SB_WIDTH, "f32+bf16"),
                  (3 * SB_WIDTH, 3 * SB_WIDTH + MEM_WIDTH, "bf16"))
_PLAN_B = ((0, SGU_WIDTH, "gelu"), (SGU_WIDTH, 2 * SGU_WIDTH, "gelu_norm"),
           (2 * SGU_WIDTH, 2 * SGU_WIDTH + MEM_WIDTH, "bf16"))
_PLAN_MEM = ((0, MEM_WIDTH, "f32+bf16"), (MEM_WIDTH, 2 * MEM_WIDTH, "f32+bf16"))


def kernel(x_prompt, x_sample, cache_sb_k, cache_sb_v, cache_mem_k, cache_mem_v,
           mem_prompt, g_mix, w_in_a, w_in_b, w_sp, b_sp, g_sgu, g_mem, w_mem_kv,
           w_out, g_ffn, w_gate, w_up, w_down, g_final):
    batch, seq, _ = x_prompt.shape
    dec_batch, dec_seq, _ = x_sample.shape
    depth = g_mix.shape[0]
    tm_p = 512
    tm_s = dec_batch * dec_seq

    y_p = x_prompt.reshape(batch * seq, D_MODEL)
    y_s = x_sample.reshape(tm_s, D_MODEL)
    mem_rows = mem_prompt.reshape(batch * N_MEM, D_MODEL)

    sb_k_p, sb_v_p, sb_k_s, sb_v_s, mem_k_p, mem_v_p, sgu_v_s = ([] for _ in range(7))
    for l in range(depth):
        mk_f, mk_b, mv_f, mv_b = _norm_proj(
            mem_rows, g_mem[l], w_mem_kv[l], _PLAN_MEM, tm=tm_p, name=f"mem_kv{l}")
        mem_k_p.append(mk_f.reshape(batch, N_MEM, N_MEM_HEADS, HEAD_DIM))
        mem_v_p.append(mv_f.reshape(batch, N_MEM, N_MEM_HEADS, HEAD_DIM))
        mk_b = mk_b.reshape(batch, N_MEM, MEM_WIDTH)
        mv_b = mv_b.reshape(batch, N_MEM, MEM_WIDTH)
        cmk = cache_mem_k[l].reshape(dec_batch, N_MEM, MEM_WIDTH).astype(BF16)
        cmv = cache_mem_v[l].reshape(dec_batch, N_MEM, MEM_WIDTH).astype(BF16)

        if l % 2 == 0:
            ia = l // 2
            q, kt_f, k_b, vt_f, v_b, qm_p = _norm_proj(
                y_p, g_mix[l], w_in_a[ia], _PLAN_A_PROMPT, tm=tm_p,
                rows_per_batch=seq, name=f"in_a{l}_p")
            sb_k_p.append(_heads_minor(kt_f, N_SB_HEADS))
            sb_v_p.append(_heads_minor(vt_f, N_SB_HEADS))
            o_p = _sb_prompt(q, k_b, v_b, batch=batch, seq=seq)

            q, k_f, k_b, v_f, v_b, qm_s = _norm_proj(
                y_s, g_mix[l], w_in_a[ia], _PLAN_A_SAMPLE, tm=tm_s, name=f"in_a{l}_s")
            sb_k_s.append(k_f.reshape(dec_batch, dec_seq, N_SB_HEADS, HEAD_DIM))
            sb_v_s.append(v_f.reshape(dec_batch, dec_seq, N_SB_HEADS, HEAD_DIM))
            o_s = _sb_decode(q, k_b, v_b, _heads_major(cache_sb_k[ia]),
                             _heads_major(cache_sb_v[ia]),
                             batch=dec_batch, t_new=dec_seq)
        else:
            ib = l // 2
            u, _, v_b, qm_p = _norm_proj(
                y_p, g_mix[l], w_in_b[ib], _PLAN_B, tm=tm_p, gate_norm=g_sgu[ib],
                name=f"in_b{l}_p")
            span = min(seq, MLP_CHUNK)
            bias = jnp.repeat(b_sp[ib][:, :span].T, HEAD_DIM, axis=1)
            o_p = _sgu(u, v_b, w_sp[ib][:, :span, :span], bias, tm=tm_p, name=f"sgu{l}_p")

            u, v_f, v_b, qm_s = _norm_proj(
                y_s, g_mix[l], w_in_b[ib], _PLAN_B, tm=tm_s, gate_norm=g_sgu[ib],
                name=f"in_b{l}_s")
            sgu_v_s.append(v_f.reshape(dec_batch, dec_seq, SGU_WIDTH))
            span = min(dec_seq, MLP_CHUNK)
            reps = MLP_CHUNK // span
            eye = jnp.eye(reps, dtype=F32)
            w_bd = jnp.einsum("ab,gts->gatbs", eye, w_sp[ib][:, :span, :span]).reshape(
                -1, MLP_CHUNK, MLP_CHUNK)
            bias = jnp.tile(jnp.repeat(b_sp[ib][:, :span].T, HEAD_DIM, axis=1), (reps, 1))
            o_s = _sgu(u, v_b, w_bd, bias, tm=tm_s, name=f"sgu{l}_s")

        y_p = _mix_out(y_p, o_p, qm_p, mk_b, mv_b, w_out[l], tm=tm_p,
                       rows_per_batch=seq, name=f"mix_out{l}_p")
        y_s = _mix_out(y_s, o_s, qm_s, cmk, cmv, w_out[l], tm=dec_seq,
                       rows_per_batch=dec_seq, name=f"mix_out{l}_s")
        last = g_final if l == depth - 1 else None
        y_p = _ffn(y_p, g_ffn[l], w_gate[l], w_up[l], w_down[l], tm=tm_p,
                   g_final=last, name=f"ffn{l}_p")
        y_s = _ffn(y_s, g_ffn[l], w_gate[l], w_up[l], w_down[l], tm=tm_s,
                   g_final=last, name=f"ffn{l}_s")

    return (y_p.reshape(batch, seq, D_MODEL), y_s.reshape(dec_batch, dec_seq, D_MODEL),
            jnp.stack(sb_k_p), jnp.stack(sb_v_p), jnp.stack(sb_k_s), jnp.stack(sb_v_s),
            jnp.stack(mem_k_p), jnp.stack(mem_v_p), jnp.stack(sgu_v_s))
```

```python
import functools

import jax
import jax.numpy as jnp
from jax import lax
from jax.experimental import pallas as pl
from jax.experimental.pallas import tpu as pltpu

F32 = jnp.float32
BF16 = jnp.bfloat16

D_MODEL = 1024
HEAD_DIM = 64
SB_WIDTH = 768
N_SB_HEADS = SB_WIDTH // HEAD_DIM
SGU_WIDTH = 768
MLP_CHUNK = 128
MEM_WIDTH = 256
N_MEM = 256
N_MEM_HEADS = MEM_WIDTH // HEAD_DIM
D_FF = 2816
EPS = 1e-6
SB_SCALE = HEAD_DIM ** -0.5
LOG2E = 1.4426950408889634

LANES = 128
V7X_VMEM_BYTES = 64 * 1024 * 1024
VMEM_LIMIT = V7X_VMEM_BYTES - 8 * 1024 * 1024

SB_DEAD_MASS = 106.0

SB_TQ = 256
SB_TK = 256
SB_PAIRS = 2
FF_CHUNK = 256
OUT_CHUNK = 256
PROJ_CHUNK = 256


def _cparams(*semantics):
    return pltpu.CompilerParams(dimension_semantics=semantics,
                                vmem_limit_bytes=VMEM_LIMIT)


def _const_spec(shape):
    zeros = (0,) * len(shape)
    return pl.BlockSpec(shape, lambda *_: zeros, pipeline_mode=pl.Buffered(1))


def _rmsnorm_rows(x, g):
    ms = jnp.mean(x * x, axis=-1, keepdims=True)
    return (x * lax.rsqrt(ms + EPS)) * g


def _dot_nt(a, b):
    return lax.dot_general(a, b, (((1,), (1,)), ((), ())), preferred_element_type=F32)


_PLAN_FORMS = {
    "bf16": ("bf16",),
    "f32": ("f32",),
    "f32+bf16": ("f32", "bf16"),
    "f32T+bf16": ("f32T", "bf16"),
    "gelu": ("f32",),
    "gelu_norm": ("f32", "bf16"),
    "gelu_norm_b": ("bf16",),
}


def _norm_proj_kernel(x_ref, g_ref, w_ref, *rest, plan, has_gate_norm):
    if has_gate_norm:
        gn_ref, out_refs = rest[0], rest[1:]
    else:
        gn_ref, out_refs = None, rest
    h = _rmsnorm_rows(x_ref[...], g_ref[...]).astype(BF16)
    tasks, k = [], 0
    for lo, hi, kind in plan:
        refs = out_refs[k:k + len(_PLAN_FORMS[kind])]
        k += len(refs)
        tasks += [(lo, hi, kind, refs, c, min(c + PROJ_CHUNK, hi))
                  for c in range(lo, hi, PROJ_CHUNK)]

    def product(task):
        a, b = task[4:]
        return jnp.dot(h, w_ref[:, a:b], preferred_element_type=F32)

    def store(refs, forms, a, b, value):
        for ref, form in zip(refs, forms):
            if form == "f32T":
                ref[a:b, :] = value.T
            else:
                ref[:, a:b] = value.astype(BF16 if form == "bf16" else F32)

    acts = []
    ahead = product(tasks[0])
    for n, task in enumerate(tasks):
        lo, hi, kind, refs, a, b = task
        p = ahead
        if n + 1 < len(tasks):
            ahead = product(tasks[n + 1])
        if kind.startswith("gelu"):
            p = jax.nn.gelu(p)
        if not kind.startswith("gelu_norm"):
            store(refs, _PLAN_FORMS[kind], a - lo, b - lo, p)
            continue
        acts.append((a - lo, b - lo, p))
        if b == hi:
            ms = sum(jnp.sum(t * t, axis=-1, keepdims=True) for _, _, t in acts) / (hi - lo)
            inv = lax.rsqrt(ms + EPS)
            for a0, b0, t in acts:
                store(refs, _PLAN_FORMS[kind], a0, b0, (t * inv) * gn_ref[:, a0:b0])
            acts = []


def _norm_proj(x, g, w, plan, *, tm, rows_per_batch=None, gate_norm=None, name):
    m, d = x.shape
    n = w.shape[1]
    out_shape, out_specs = [], []
    for lo, hi, kind in plan:
        for form in _PLAN_FORMS[kind]:
            if form == "f32T":
                per = rows_per_batch // tm
                out_shape.append(jax.ShapeDtypeStruct(
                    (m // rows_per_batch, hi - lo, rows_per_batch), F32))
                out_specs.append(pl.BlockSpec(
                    (None, hi - lo, tm), lambda i, per=per: (i // per, 0, i % per)))
            else:
                out_shape.append(jax.ShapeDtypeStruct(
                    (m, hi - lo), BF16 if form == "bf16" else F32))
                out_specs.append(pl.BlockSpec((tm, hi - lo), lambda i: (i, 0)))
    in_specs = [pl.BlockSpec((tm, d), lambda i: (i, 0)),
                _const_spec((1, d)), _const_spec((d, n))]
    args = [x, g.reshape(1, d), w.astype(BF16)]
    if gate_norm is not None:
        in_specs.append(_const_spec((1, gate_norm.shape[-1])))
        args.append(gate_norm.reshape(1, -1))
    return pl.pallas_call(
        functools.partial(_norm_proj_kernel, plan=plan,
                          has_gate_norm=gate_norm is not None),
        out_shape=out_shape, grid=(m // tm,), in_specs=in_specs,
        out_specs=out_specs, compiler_params=_cparams("arbitrary"), name=name,
    )(*args)


def _upper2(n):
    j = lax.broadcasted_iota(jnp.int32, (2 * n, n), 0) % n
    s = lax.broadcasted_iota(jnp.int32, (2 * n, n), 1)
    return jnp.where(j >= s, 1.0, 0.0).astype(BF16)


def _sb_weights(z, upper2, mass, causal):
    soft = jnp.maximum(z, 0.0) + jnp.log(1.0 + jnp.exp2(jnp.abs(z) * -LOG2E))
    if causal is not None:
        soft = jnp.where(causal, soft, 0.0)
    hi = soft.astype(BF16)
    lo = (soft - hi.astype(F32)).astype(BF16)
    from_here = jnp.dot(jnp.concatenate([hi, lo], axis=1), upper2,
                        preferred_element_type=F32) + mass
    a = jnp.exp2((z - from_here) * LOG2E)
    if causal is not None:
        a = jnp.where(causal, a, 0.0)
    return a.astype(BF16), mass + jnp.sum(soft, axis=-1, keepdims=True)


def _sb_prompt_kernel(q_ref, k_ref, v_ref, o_ref, *, tq, tk, pairs):
    i = pl.program_id(2)
    lane = lax.broadcasted_iota(jnp.int32, (1, LANES), 1)
    head0 = lane < HEAD_DIM
    upper2 = _upper2(tk)
    row = lax.broadcasted_iota(jnp.int32, (2 * tq, tk), 0) % tq
    col = lax.broadcasted_iota(jnp.int32, (2 * tq, tk), 1)
    has_prev = i > 0
    j_prev = jnp.maximum(i - 1, 0)

    def sweeper(p):
        lanes = slice(p * LANES, (p + 1) * LANES)
        q = q_ref[:, lanes] * jnp.asarray(SB_SCALE, BF16)
        zq = jnp.zeros_like(q)
        q2 = jnp.concatenate([jnp.where(head0, q, zq), jnp.where(head0, zq, q)], axis=0)

        def logits(j):
            return _dot_nt(q2, k_ref[pl.ds(pl.multiple_of(j * tk, tk), tk), lanes])

        def values(a, j):
            vb = v_ref[pl.ds(pl.multiple_of(j * tk, tk), tk), lanes]
            zv = jnp.zeros_like(vb)
            return (jnp.dot(a[:tq], jnp.where(head0, vb, zv), preferred_element_type=F32)
                    + jnp.dot(a[tq:], jnp.where(head0, zv, vb), preferred_element_type=F32))

        return logits, values

    states = []
    for p in range(pairs):
        logits, values = sweeper(p)
        z_diag, z_prev = logits(i), logits(j_prev)
        a_diag, mass_diag = _sb_weights(z_diag, upper2, jnp.zeros((2 * tq, 1), F32),
                                        col < row)
        a_prev, mass_prev = _sb_weights(z_prev, upper2, mass_diag, None)
        acc = values(a_diag, i) + jnp.where(has_prev, values(a_prev, j_prev), 0.0)
        states.append((acc, jnp.where(has_prev, mass_prev, mass_diag)))

    def cond(state):
        j, _, _, least = state
        return jnp.logical_and(j >= 0, least < SB_DEAD_MASS)

    for p, (acc, mass) in enumerate(states):
        logits, values = sweeper(p)

        def body(state, logits=logits, values=values):
            j, acc, mass, _ = state
            a, mass = _sb_weights(logits(j), upper2, mass, None)
            return j - 1, acc + values(a, j), mass, jnp.min(mass)

        _, acc, _, _ = lax.while_loop(cond, body, (i - 2, acc, mass, jnp.min(mass)))
        o_ref[:, p * LANES:(p + 1) * LANES] = acc.astype(o_ref.dtype)


def _sb_prompt(q, k, v, *, batch, seq):
    tq, tk, pairs = SB_TQ, SB_TK, SB_PAIRS
    assert tq == tk and seq % tq == 0
    q3, k3, v3 = (a.reshape(batch, seq, SB_WIDTH) for a in (q, k, v))
    width = pairs * LANES
    q_spec = pl.BlockSpec((None, tq, width), lambda b, p, i: (b, i, p))
    kv_spec = pl.BlockSpec((None, seq, width), lambda b, p, i: (b, 0, p))
    out = pl.pallas_call(
        functools.partial(_sb_prompt_kernel, tq=tq, tk=tk, pairs=pairs),
        out_shape=jax.ShapeDtypeStruct((batch, seq, SB_WIDTH), BF16),
        grid=(batch, SB_WIDTH // width, seq // tq),
        in_specs=[q_spec, kv_spec, kv_spec], out_specs=q_spec,
        compiler_params=_cparams("arbitrary", "arbitrary", "arbitrary"),
        name="sb_prompt",
    )(q3, k3, v3)
    return out.reshape(batch * seq, SB_WIDTH)


def _sb_decode_kernel(q_ref, kn_ref, vn_ref, kc_hbm, vc_hbm, o_ref,
                      k_buf, v_buf, sems, acc_ref, *, t_new, tk, n_blocks):
    b = pl.program_id(0)
    n_rows = N_SB_HEADS * t_new

    def block_copies(j, slot):
        keys = pl.ds(pl.multiple_of(j * tk, tk), tk)
        return (pltpu.make_async_copy(kc_hbm.at[b, :, keys], k_buf.at[slot], sems.at[0, slot]),
                pltpu.make_async_copy(vc_hbm.at[b, :, keys], v_buf.at[slot], sems.at[1, slot]))

    def fetch(j, slot):
        for copy in block_copies(j, slot):
            copy.start()

    def arrive(j, slot):
        for copy in block_copies(j, slot):
            copy.wait()

    last = n_blocks - 1
    fetch(last, last % 2)

    row_head = lax.broadcasted_iota(jnp.int32, (n_rows, SB_WIDTH), 0) // t_new
    col_head = lax.broadcasted_iota(jnp.int32, (n_rows, SB_WIDTH), 1) // HEAD_DIM
    q_rows = jnp.concatenate([q_ref[...]] * N_SB_HEADS, axis=0)
    q_bd = jnp.where(row_head == col_head, q_rows * SB_SCALE, 0.0).astype(BF16)

    n_pad = kn_ref.shape[0]
    q_idx = lax.broadcasted_iota(jnp.int32, (n_rows, n_pad), 0) % t_new
    s_idx = lax.broadcasted_iota(jnp.int32, (n_rows, n_pad), 1)
    a, mass = _sb_weights(_dot_nt(q_bd, kn_ref[...]), _upper2(n_pad),
                          jnp.zeros((n_rows, 1), F32), s_idx < q_idx)
    acc_ref[...] = jnp.dot(a, vn_ref[...], preferred_element_type=F32)
    upper2 = _upper2(tk)

    def cond(state):
        j, _, least = state
        return jnp.logical_and(j >= 0, least < SB_DEAD_MASS)

    def body(state):
        j, mass, _ = state
        slot = lax.rem(j, 2)
        arrive(j, slot)

        @pl.when(j > 0)
        def _():
            fetch(j - 1, 1 - slot)

        z = jnp.dot(q_bd, k_buf[slot].astype(BF16), preferred_element_type=F32)
        a, mass = _sb_weights(z, upper2, mass, None)
        acc_ref[...] += _dot_nt(a, v_buf[slot].astype(BF16))
        return j - 1, mass, jnp.min(mass)

    j_end, _, _ = lax.while_loop(cond, body, (last, mass, jnp.min(mass)))

    @pl.when(j_end >= 0)
    def _():
        arrive(j_end, lax.rem(j_end, 2))

    out_head = lax.broadcasted_iota(jnp.int32, (t_new, SB_WIDTH), 1) // HEAD_DIM
    out = jnp.zeros((t_new, SB_WIDTH), F32)
    for h in range(N_SB_HEADS):
        out = out + jnp.where(out_head == h, acc_ref[h * t_new:(h + 1) * t_new, :], 0.0)
    o_ref[...] = out.astype(o_ref.dtype)


def _sb_decode(q, k_new, v_new, cache_kt, cache_vt, *, batch, t_new):
    past = cache_kt.shape[2]
    tk = SB_TK
    assert past % tk == 0
    pad = ((0, 0), (0, LANES - t_new), (0, 0))
    kn = jnp.pad(k_new.reshape(batch, t_new, SB_WIDTH), pad)
    vn = jnp.pad(v_new.reshape(batch, t_new, SB_WIDTH), pad)
    row_spec = pl.BlockSpec((None, t_new, SB_WIDTH), lambda b: (b, 0, 0))
    new_spec = pl.BlockSpec((None, LANES, SB_WIDTH), lambda b: (b, 0, 0))
    hbm_spec = pl.BlockSpec(memory_space=pl.ANY)
    out = pl.pallas_call(
        functools.partial(_sb_decode_kernel, t_new=t_new, tk=tk, n_blocks=past // tk),
        out_shape=jax.ShapeDtypeStruct((batch, t_new, SB_WIDTH), BF16),
        grid=(batch,),
        in_specs=[row_spec, new_spec, new_spec, hbm_spec, hbm_spec],
        out_specs=row_spec,
        scratch_shapes=[pltpu.VMEM((2, SB_WIDTH, tk), F32),
                        pltpu.VMEM((2, SB_WIDTH, tk), F32),
                        pltpu.SemaphoreType.DMA((2, 2)),
                        pltpu.VMEM((N_SB_HEADS * t_new, SB_WIDTH), F32)],
        compiler_params=_cparams("arbitrary"),
        name="sb_decode",
    )(q.reshape(batch, t_new, SB_WIDTH), kn, vn, cache_kt, cache_vt)
    return out.reshape(batch * t_new, SB_WIDTH)


def _heads_major(x):
    b, t, h, d = x.shape
    return jnp.transpose(x, (0, 2, 3, 1)).reshape(b, h * d, t)


def _heads_minor(xt, heads):
    b, hd, t = xt.shape
    return jnp.transpose(xt.reshape(b, heads, hd // heads, t), (0, 3, 1, 2))


def _sgu_kernel(u_ref, v_ref, w_ref, b_ref, o_ref, *, span, n_chunks):
    lane = lax.broadcasted_iota(jnp.int32, (1, LANES), 1)
    group0 = lane < HEAD_DIM
    t = lax.broadcasted_iota(jnp.int32, (span, span), 0)
    s = lax.broadcasted_iota(jnp.int32, (span, span), 1)
    tril = s <= t
    for p in range(SGU_WIDTH // LANES):
        w_pair = jnp.concatenate(
            [jnp.where(tril, w_ref[2 * p], 0.0), jnp.where(tril, w_ref[2 * p + 1], 0.0)],
            axis=1).astype(BF16)
        cols = slice(p * LANES, (p + 1) * LANES)
        for c in range(n_chunks):
            rows = slice(c * span, (c + 1) * span)
            vv = v_ref[rows, cols]
            zv = jnp.zeros_like(vv)
            v_pair = jnp.concatenate([jnp.where(group0, vv, zv),
                                      jnp.where(group0, zv, vv)], axis=0)
            mixed = jnp.dot(w_pair, v_pair, preferred_element_type=F32) + b_ref[:, cols]
            o_ref[rows, cols] = (u_ref[rows, cols] * mixed).astype(o_ref.dtype)


def _sgu(u, v, w, bias, *, tm, name):
    m = u.shape[0]
    span = w.shape[-1]
    blk = pl.BlockSpec((tm, SGU_WIDTH), lambda i: (i, 0))
    return pl.pallas_call(
        functools.partial(_sgu_kernel, span=span, n_chunks=tm // span),
        out_shape=jax.ShapeDtypeStruct((m, SGU_WIDTH), BF16),
        grid=(m // tm,),
        in_specs=[blk, blk, _const_spec(w.shape), _const_spec(bias.shape)],
        out_specs=blk, compiler_params=_cparams("arbitrary"), name=name,
    )(u, v, w, bias)


def _mix_out_kernel(y_ref, o_ref, qm_ref, mk_ref, mv_ref, w_ref, out_ref):
    lane = lax.broadcasted_iota(jnp.int32, (1, MEM_WIDTH), 1) // HEAD_DIM
    qm = qm_ref[...] * jnp.asarray(SB_SCALE, BF16)
    zq = jnp.zeros_like(qm)
    mk = mk_ref[...]
    mv = mv_ref[...]
    zv = jnp.zeros_like(mv)
    scores = [_dot_nt(jnp.where(lane == h, qm, zq), mk) for h in range(N_MEM_HEADS)]
    width = o_ref.shape[1]
    o = o_ref[...]
    chunks = [slice(c * OUT_CHUNK, (c + 1) * OUT_CHUNK) for c in range(D_MODEL // OUT_CHUNK)]
    mixer_part = [jnp.dot(o, w_ref[:width, cols], preferred_element_type=F32)
                  for cols in chunks]
    mem = 0.0
    for h, s in enumerate(scores):
        e = jnp.exp(s - jnp.max(s, axis=-1, keepdims=True))
        p = e / jnp.sum(e, axis=-1, keepdims=True)
        mem = mem + jnp.dot(p.astype(BF16), jnp.where(lane == h, mv, zv),
                            preferred_element_type=F32)
    mem = mem.astype(BF16)
    for cols, part in zip(chunks, mixer_part):
        out_ref[:, cols] = (y_ref[:, cols] + part) + jnp.dot(
            mem, w_ref[width:, cols], preferred_element_type=F32)


def _mix_out(y, o, qm, mem_k, mem_v, w_out, *, tm, rows_per_batch, name):
    m = y.shape[0]
    assert rows_per_batch % tm == 0
    per = rows_per_batch // tm
    mem_spec = pl.BlockSpec((None, N_MEM, MEM_WIDTH), lambda i: (i // per, 0, 0))
    return pl.pallas_call(
        _mix_out_kernel,
        out_shape=jax.ShapeDtypeStruct((m, D_MODEL), F32),
        grid=(m // tm,),
        in_specs=[pl.BlockSpec((tm, D_MODEL), lambda i: (i, 0)),
                  pl.BlockSpec((tm, SB_WIDTH), lambda i: (i, 0)),
                  pl.BlockSpec((tm, MEM_WIDTH), lambda i: (i, 0)),
                  mem_spec, mem_spec, _const_spec((D_MODEL, D_MODEL))],
        out_specs=pl.BlockSpec((tm, D_MODEL), lambda i: (i, 0)),
        compiler_params=_cparams("arbitrary"), name=name,
    )(y, o, qm, mem_k, mem_v, w_out.astype(BF16))


def _ffn_kernel(y_ref, g_ref, wg_ref, wu_ref, wd_ref, *rest, final):
    if final:
        gf_ref, out_ref = rest
    else:
        (out_ref,) = rest
    y = y_ref[...]
    h = _rmsnorm_rows(y, g_ref[...]).astype(BF16)
    acc = y
    for c in range(D_FF // FF_CHUNK):
        cols = slice(c * FF_CHUNK, (c + 1) * FF_CHUNK)
        gate = jnp.dot(h, wg_ref[:, cols], preferred_element_type=F32)
        up = jnp.dot(h, wu_ref[:, cols], preferred_element_type=F32)
        act = (jax.nn.silu(gate) * up).astype(BF16)
        acc = acc + jnp.dot(act, wd_ref[cols, :], preferred_element_type=F32)
    if final:
        acc = _rmsnorm_rows(acc, gf_ref[...])
    out_ref[...] = acc


def _ffn(y, g, w_gate, w_up, w_down, *, tm, g_final=None, name):
    m = y.shape[0]
    row = pl.BlockSpec((tm, D_MODEL), lambda i: (i, 0))
    in_specs = [row, _const_spec((1, D_MODEL)), _const_spec((D_MODEL, D_FF)),
                _const_spec((D_MODEL, D_FF)), _const_spec((D_FF, D_MODEL))]
    args = [y, g.reshape(1, D_MODEL), w_gate.astype(BF16), w_up.astype(BF16),
            w_down.astype(BF16)]
    if g_final is not None:
        in_specs.append(_const_spec((1, D_MODEL)))
        args.append(g_final.reshape(1, D_MODEL))
    return pl.pallas_call(
        functools.partial(_ffn_kernel, final=g_final is not None),
        out_shape=jax.ShapeDtypeStruct((m, D_MODEL), F32),
        grid=(m // tm,), in_specs=in_specs, out_specs=row,
        compiler_params=_cparams("arbitrary"), name=name,
    )(*args)


_PLAN_A_PROMPT = ((0, SB_WIDTH, "bf16"), (SB_WIDTH, 2 * SB_WIDTH, "f32T+bf16"),
                  (2 * SB_WIDTH, 3 * SB_WIDTH, "f32T+bf16"),
                  (3 * SB_WIDTH, 3 * SB_WIDTH + MEM_WIDTH, "bf16"))
_PLAN_A_SAMPLE = ((0, SB_WIDTH, "f32"), (SB_WIDTH, 2 * SB_WIDTH, "f32+bf16"),
                  (2 * SB_WIDTH, 3 * SB_WIDTH, "f32+bf16"),
                  (3 * SB_WIDTH, 3 * SB_WIDTH + MEM_WIDTH, "bf16"))
_PLAN_B_PROMPT = ((SGU_WIDTH, 2 * SGU_WIDTH, "gelu_norm_b"), (0, SGU_WIDTH, "gelu"),
                  (2 * SGU_WIDTH, 2 * SGU_WIDTH + MEM_WIDTH, "bf16"))
_PLAN_B_SAMPLE = ((SGU_WIDTH, 2 * SGU_WIDTH, "gelu_norm"),) + _PLAN_B_PROMPT[1:]
_PLAN_MEM = ((0, MEM_WIDTH, "f32+bf16"), (MEM_WIDTH, 2 * MEM_WIDTH, "f32+bf16"))


def kernel(x_prompt, x_sample, cache_sb_k, cache_sb_v, cache_mem_k, cache_mem_v,
           mem_prompt, g_mix, w_in_a, w_in_b, w_sp, b_sp, g_sgu, g_mem, w_mem_kv,
           w_out, g_ffn, w_gate, w_up, w_down, g_final):
    batch, seq, _ = x_prompt.shape
    dec_batch, dec_seq, _ = x_sample.shape
    depth = g_mix.shape[0]
    tm_p = 512
    tm_s = dec_batch * dec_seq

    y_p = x_prompt.reshape(batch * seq, D_MODEL)
    y_s = x_sample.reshape(tm_s, D_MODEL)
    mem_rows = mem_prompt.reshape(batch * N_MEM, D_MODEL)

    sb_k_p, sb_v_p, sb_k_s, sb_v_s, mem_k_p, mem_v_p, sgu_v_s = ([] for _ in range(7))
    for l in range(depth):
        mk_f, mk_b, mv_f, mv_b = _norm_proj(
            mem_rows, g_mem[l], w_mem_kv[l], _PLAN_MEM, tm=tm_p, name=f"mem_kv{l}")
        mem_k_p.append(mk_f.reshape(batch, N_MEM, N_MEM_HEADS, HEAD_DIM))
        mem_v_p.append(mv_f.reshape(batch, N_MEM, N_MEM_HEADS, HEAD_DIM))
        mk_b = mk_b.reshape(batch, N_MEM, MEM_WIDTH)
        mv_b = mv_b.reshape(batch, N_MEM, MEM_WIDTH)
        cmk = cache_mem_k[l].reshape(dec_batch, N_MEM, MEM_WIDTH).astype(BF16)
        cmv = cache_mem_v[l].reshape(dec_batch, N_MEM, MEM_WIDTH).astype(BF16)

        if l % 2 == 0:
            ia = l // 2
            q, kt_f, k_b, vt_f, v_b, qm_p = _norm_proj(
                y_p, g_mix[l], w_in_a[ia], _PLAN_A_PROMPT, tm=tm_p,
                rows_per_batch=seq, name=f"in_a{l}_p")
            sb_k_p.append(_heads_minor(kt_f, N_SB_HEADS))
            sb_v_p.append(_heads_minor(vt_f, N_SB_HEADS))
            o_p = _sb_prompt(q, k_b, v_b, batch=batch, seq=seq)

            q, k_f, k_b, v_f, v_b, qm_s = _norm_proj(
                y_s, g_mix[l], w_in_a[ia], _PLAN_A_SAMPLE, tm=tm_s, name=f"in_a{l}_s")
            sb_k_s.append(k_f.reshape(dec_batch, dec_seq, N_SB_HEADS, HEAD_DIM))
            sb_v_s.append(v_f.reshape(dec_batch, dec_seq, N_SB_HEADS, HEAD_DIM))
            o_s = _sb_decode(q, k_b, v_b, _heads_major(cache_sb_k[ia]),
                             _heads_major(cache_sb_v[ia]),
                             batch=dec_batch, t_new=dec_seq)
        else:
            ib = l // 2
            v_b, u, qm_p = _norm_proj(
                y_p, g_mix[l], w_in_b[ib], _PLAN_B_PROMPT, tm=tm_p, gate_norm=g_sgu[ib],
                name=f"in_b{l}_p")
            span = min(seq, MLP_CHUNK)
            bias = jnp.repeat(b_sp[ib][:, :span].T, HEAD_DIM, axis=1)
            o_p = _sgu(u, v_b, w_sp[ib][:, :span, :span], bias, tm=tm_p, name=f"sgu{l}_p")

            v_f, v_b, u, qm_s = _norm_proj(
                y_s, g_mix[l], w_in_b[ib], _PLAN_B_SAMPLE, tm=tm_s, gate_norm=g_sgu[ib],
                name=f"in_b{l}_s")
            sgu_v_s.append(v_f.reshape(dec_batch, dec_seq, SGU_WIDTH))
            span = min(dec_seq, MLP_CHUNK)
            reps = MLP_CHUNK // span
            eye = jnp.eye(reps, dtype=F32)
            w_bd = jnp.einsum("ab,gts->gatbs", eye, w_sp[ib][:, :span, :span]).reshape(
                -1, MLP_CHUNK, MLP_CHUNK)
            bias = jnp.tile(jnp.repeat(b_sp[ib][:, :span].T, HEAD_DIM, axis=1), (reps, 1))
            o_s = _sgu(u, v_b, w_bd, bias, tm=tm_s, name=f"sgu{l}_s")

        y_p = _mix_out(y_p, o_p, qm_p, mk_b, mv_b, w_out[l], tm=tm_p,
                       rows_per_batch=seq, name=f"mix_out{l}_p")
        y_s = _mix_out(y_s, o_s, qm_s, cmk, cmv, w_out[l], tm=dec_seq,
                       rows_per_batch=dec_seq, name=f"mix_out{l}_s")
        last = g_final if l == depth - 1 else None
        y_p = _ffn(y_p, g_ffn[l], w_gate[l], w_up[l], w_down[l], tm=tm_p,
                   g_final=last, name=f"ffn{l}_p")
        y_s = _ffn(y_s, g_ffn[l], w_gate[l], w_up[l], w_down[l], tm=tm_s,
                   g_final=last, name=f"ffn{l}_s")

    return (y_p.reshape(batch, seq, D_MODEL), y_s.reshape(dec_batch, dec_seq, D_MODEL),
            jnp.stack(sb_k_p), jnp.stack(sb_v_p), jnp.stack(sb_k_s), jnp.stack(sb_v_s),
            jnp.stack(mem_k_p), jnp.stack(mem_v_p), jnp.stack(sgu_v_s))
```

```python
import functools

import jax
import jax.numpy as jnp
from jax import lax
from jax.experimental import pallas as pl
from jax.experimental.pallas import tpu as pltpu

F32 = jnp.float32
BF16 = jnp.bfloat16

D_MODEL = 1024
HEAD_DIM = 64
SB_WIDTH = 768
N_SB_HEADS = SB_WIDTH // HEAD_DIM
SGU_WIDTH = 768
MLP_CHUNK = 128
MEM_WIDTH = 256
N_MEM = 256
N_MEM_HEADS = MEM_WIDTH // HEAD_DIM
D_FF = 2816
EPS = 1e-6
SB_SCALE = HEAD_DIM ** -0.5
LOG2E = 1.4426950408889634

LANES = 128
V7X_VMEM_BYTES = 64 * 1024 * 1024
VMEM_LIMIT = V7X_VMEM_BYTES - 8 * 1024 * 1024

SB_DEAD_MASS = 106.0

SB_TQ = 256
SB_TK = 256
SB_PAIRS = 6
SB_EARLY = 176
FF_CHUNK = 256
OUT_CHUNK = 256
PROJ_CHUNK = 256


def _cparams(*semantics):
    return pltpu.CompilerParams(dimension_semantics=semantics,
                                vmem_limit_bytes=VMEM_LIMIT)


def _const_spec(shape):
    zeros = (0,) * len(shape)
    return pl.BlockSpec(shape, lambda *_: zeros, pipeline_mode=pl.Buffered(1))


def _layer_spec(stacked, layer):
    return pl.BlockSpec((None,) + stacked.shape[1:], lambda *_: (layer, 0, 0),
                        pipeline_mode=pl.Buffered(1))


def _as_rows(g):
    return g.reshape(g.shape[0], 1, g.shape[1])


def _rmsnorm_rows(x, g):
    ms = jnp.mean(x * x, axis=-1, keepdims=True)
    return (x * lax.rsqrt(ms + EPS)) * g


def _dot_nt(a, b):
    return lax.dot_general(a, b, (((1,), (1,)), ((), ())), preferred_element_type=F32)


_PLAN_FORMS = {
    "bf16": ("bf16",),
    "f32": ("f32",),
    "f32+bf16": ("f32", "bf16"),
    "f32T": ("f32T",),
    "f32T+bf16": ("f32T", "bf16"),
    "gelu": ("f32",),
    "gelu_norm": ("f32", "bf16"),
    "gelu_norm_b": ("bf16",),
}


def _norm_proj_kernel(x_ref, g_ref, w_ref, *rest, plan, has_gate_norm):
    if has_gate_norm:
        gn_ref, out_refs = rest[0], rest[1:]
    else:
        gn_ref, out_refs = None, rest
    h = _rmsnorm_rows(x_ref[...], g_ref[...]).astype(BF16)
    tasks, k = [], 0
    for lo, hi, kind in plan:
        refs = out_refs[k:k + len(_PLAN_FORMS[kind])]
        k += len(refs)
        tasks += [(lo, hi, kind, refs, c, min(c + PROJ_CHUNK, hi))
                  for c in range(lo, hi, PROJ_CHUNK)]

    def product(task):
        a, b = task[4:]
        return jnp.dot(h, w_ref[:, a:b], preferred_element_type=F32)

    def store(refs, forms, a, b, value):
        for ref, form in zip(refs, forms):
            if form == "f32T":
                ref[a:b, :] = value.T
            else:
                ref[:, a:b] = value.astype(BF16 if form == "bf16" else F32)

    acts = []
    ahead = product(tasks[0])
    for n, task in enumerate(tasks):
        lo, hi, kind, refs, a, b = task
        p = ahead
        if n + 1 < len(tasks):
            ahead = product(tasks[n + 1])
        if kind.startswith("gelu"):
            p = jax.nn.gelu(p)
        if not kind.startswith("gelu_norm"):
            store(refs, _PLAN_FORMS[kind], a - lo, b - lo, p)
            continue
        acts.append((a - lo, b - lo, p))
        if b == hi:
            ms = sum(jnp.sum(t * t, axis=-1, keepdims=True) for _, _, t in acts) / (hi - lo)
            inv = lax.rsqrt(ms + EPS)
            for a0, b0, t in acts:
                store(refs, _PLAN_FORMS[kind], a0, b0, (t * inv) * gn_ref[:, a0:b0])
            acts = []


def _norm_proj(x, g, w, plan, *, tm, rows_per_batch=None, gate_norm=None, name):
    m, d = x.shape
    out_shape, out_specs = [], []
    for lo, hi, kind in plan:
        for form in _PLAN_FORMS[kind]:
            if form == "f32T":
                per = rows_per_batch // tm
                out_shape.append(jax.ShapeDtypeStruct(
                    (m // rows_per_batch, hi - lo, rows_per_batch), F32))
                out_specs.append(pl.BlockSpec(
                    (None, hi - lo, tm), lambda i, per=per: (i // per, 0, i % per)))
            else:
                out_shape.append(jax.ShapeDtypeStruct(
                    (m, hi - lo), BF16 if form == "bf16" else F32))
                out_specs.append(pl.BlockSpec((tm, hi - lo), lambda i: (i, 0)))
    layered = [g, w] + ([gate_norm] if gate_norm is not None else [])
    in_specs = [pl.BlockSpec((tm, d), lambda i: (i, 0))] + [
        _layer_spec(arr, layer) for arr, layer in layered]
    args = [x] + [arr for arr, _ in layered]
    return pl.pallas_call(
        functools.partial(_norm_proj_kernel, plan=plan,
                          has_gate_norm=gate_norm is not None),
        out_shape=out_shape, grid=(m // tm,), in_specs=in_specs,
        out_specs=out_specs, compiler_params=_cparams("arbitrary"), name=name,
    )(*args)


def _upper2(n):
    j = lax.broadcasted_iota(jnp.int32, (2 * n, n), 0) % n
    s = lax.broadcasted_iota(jnp.int32, (2 * n, n), 1)
    return jnp.where(j >= s, 1.0, 0.0).astype(BF16)


def _sb_weights(z, upper2, mass, causal):
    soft = jnp.maximum(z, 0.0) + jnp.log(1.0 + jnp.exp2(jnp.abs(z) * -LOG2E))
    if causal is not None:
        soft = jnp.where(causal, soft, 0.0)
    hi = soft.astype(BF16)
    lo = (soft - hi.astype(F32)).astype(BF16)
    from_here = jnp.dot(jnp.concatenate([hi, lo], axis=1), upper2,
                        preferred_element_type=F32) + mass
    a = jnp.exp2((z - from_here) * LOG2E)
    if causal is not None:
        a = jnp.where(causal, a, 0.0)
    return a.astype(BF16), mass + jnp.sum(soft, axis=-1, keepdims=True)


def _sb_prompt_kernel(q_ref, k_ref, v_ref, o_ref, acc_ref, mass_ref, *,
                      tq, tk, pairs, early):
    i = pl.program_id(2)
    lane = lax.broadcasted_iota(jnp.int32, (1, LANES), 1)
    head0 = lane < HEAD_DIM
    upper2 = _upper2(tk)
    row = lax.broadcasted_iota(jnp.int32, (2 * tq, tk), 0) % tq
    col = lax.broadcasted_iota(jnp.int32, (2 * tq, tk), 1)
    has_prev = i > 0
    j_prev = jnp.maximum(i - 1, 0)

    def both_heads(x, lo, hi):
        return jnp.concatenate([x[lo:hi], x[tq + lo:tq + hi]], axis=0)

    def sweeper(p):
        lanes = slice(p * LANES, (p + 1) * LANES)
        q = q_ref[:, lanes] * jnp.asarray(SB_SCALE, BF16)
        zq = jnp.zeros_like(q)
        q2 = jnp.concatenate([jnp.where(head0, q, zq), jnp.where(head0, zq, q)], axis=0)

        def logits(qs, j):
            return _dot_nt(qs, k_ref[pl.ds(pl.multiple_of(j * tk, tk), tk), lanes])

        def values(a, j):
            n = a.shape[0] // 2
            vb = v_ref[pl.ds(pl.multiple_of(j * tk, tk), tk), lanes]
            zv = jnp.zeros_like(vb)
            return (jnp.dot(a[:n], jnp.where(head0, vb, zv), preferred_element_type=F32)
                    + jnp.dot(a[n:], jnp.where(head0, zv, vb), preferred_element_type=F32))

        return q2, logits, values

    late = tq - early
    least_early = least_late = jnp.asarray(jnp.inf, F32)
    for p in range(pairs):
        q2, logits, values = sweeper(p)
        z_diag, z_prev = logits(q2, i), logits(both_heads(q2, 0, early), j_prev)
        a_diag, mass_diag = _sb_weights(z_diag, upper2, jnp.zeros((2 * tq, 1), F32),
                                        col < row)
        mass_e = both_heads(mass_diag, 0, early)
        a_prev, mass_prev = _sb_weights(z_prev, upper2, mass_e, None)
        acc = values(a_diag, i)
        acc_early = acc[:early] + jnp.where(has_prev, values(a_prev, j_prev), 0.0)
        acc_ref[:, p * LANES:(p + 1) * LANES] = jnp.concatenate(
            [acc_early, acc[early:]], axis=0)
        mass_e = jnp.where(has_prev, mass_prev, mass_e)
        mass_l = both_heads(mass_diag, early, tq)
        mass_ref[p] = jnp.concatenate(
            [mass_e[:early], mass_l[:late], mass_e[early:], mass_l[late:]], axis=0)
        least_early = jnp.minimum(least_early, jnp.min(mass_e))
        least_late = jnp.minimum(least_late, jnp.min(mass_l))

    late_row = lax.broadcasted_iota(jnp.int32, (2 * tq, 1), 0) % tq >= early

    def cond(state):
        j, least = state
        return jnp.logical_and(j >= 0, least < SB_DEAD_MASS)

    def body(state):
        j, _ = state
        fresh = jnp.logical_or(late_row, j != i - 1)
        least = jnp.asarray(jnp.inf, F32)
        for p in range(pairs):
            q2, logits, values = sweeper(p)
            mass_old = mass_ref[p]
            a, mass = _sb_weights(logits(q2, j), upper2, mass_old, None)
            a = jnp.where(fresh, a, jnp.zeros_like(a))
            mass = jnp.where(fresh, mass, mass_old)
            acc_ref[:, p * LANES:(p + 1) * LANES] += values(a, j)
            mass_ref[p] = mass
            least = jnp.minimum(least, jnp.min(mass))
        return j - 1, least

    j_start = jnp.where(least_late < SB_DEAD_MASS, i - 1, i - 2)
    lax.while_loop(cond, body, (j_start, jnp.minimum(least_early, least_late)))
    o_ref[...] = acc_ref[...].astype(o_ref.dtype)


def _sb_prompt(q, k, v, *, batch, seq):
    tq, tk, pairs = SB_TQ, SB_TK, SB_PAIRS
    assert tq == tk and seq % tq == 0
    q3, k3, v3 = (a.reshape(batch, seq, SB_WIDTH) for a in (q, k, v))
    width = pairs * LANES
    q_spec = pl.BlockSpec((None, tq, width), lambda b, p, i: (b, i, p))
    kv_spec = pl.BlockSpec((None, seq, width), lambda b, p, i: (b, 0, p))
    out = pl.pallas_call(
        functools.partial(_sb_prompt_kernel, tq=tq, tk=tk, pairs=pairs, early=SB_EARLY),
        out_shape=jax.ShapeDtypeStruct((batch, seq, SB_WIDTH), BF16),
        grid=(batch, SB_WIDTH // width, seq // tq),
        in_specs=[q_spec, kv_spec, kv_spec], out_specs=q_spec,
        scratch_shapes=[pltpu.VMEM((tq, width), F32),
                        pltpu.VMEM((pairs, 2 * tq, 1), F32)],
        compiler_params=_cparams("arbitrary", "arbitrary", "arbitrary"),
        name="sb_prompt",
    )(q3, k3, v3)
    return out.reshape(batch * seq, SB_WIDTH)


def _sb_decode_kernel(q_ref, kn_ref, vn_ref, kc_hbm, vc_hbm, o_ref,
                      k_buf, v_buf, sems, acc_ref, *, t_new, tk, n_blocks):
    b = pl.program_id(0)
    n_rows = N_SB_HEADS * t_new

    def block_copies(j, slot):
        keys = pl.ds(pl.multiple_of(j * tk, tk), tk)
        return (pltpu.make_async_copy(kc_hbm.at[b, :, keys], k_buf.at[slot], sems.at[0, slot]),
                pltpu.make_async_copy(vc_hbm.at[b, :, keys], v_buf.at[slot], sems.at[1, slot]))

    def fetch(j, slot):
        for copy in block_copies(j, slot):
            copy.start()

    def arrive(j, slot):
        for copy in block_copies(j, slot):
            copy.wait()

    last = n_blocks - 1
    fetch(last, last % 2)

    row_head = lax.broadcasted_iota(jnp.int32, (n_rows, SB_WIDTH), 0) // t_new
    col_head = lax.broadcasted_iota(jnp.int32, (n_rows, SB_WIDTH), 1) // HEAD_DIM
    q_rows = jnp.concatenate([q_ref[...]] * N_SB_HEADS, axis=0)
    q_bd = jnp.where(row_head == col_head, q_rows * SB_SCALE, 0.0).astype(BF16)

    n_pad = kn_ref.shape[0]
    q_idx = lax.broadcasted_iota(jnp.int32, (n_rows, n_pad), 0) % t_new
    s_idx = lax.broadcasted_iota(jnp.int32, (n_rows, n_pad), 1)
    a, mass = _sb_weights(_dot_nt(q_bd, kn_ref[...]), _upper2(n_pad),
                          jnp.zeros((n_rows, 1), F32), s_idx < q_idx)
    acc_ref[...] = jnp.dot(a, vn_ref[...], preferred_element_type=F32)
    upper2 = _upper2(tk)

    def cond(state):
        j, _, least = state
        return jnp.logical_and(j >= 0, least < SB_DEAD_MASS)

    def body(state):
        j, mass, _ = state
        slot = lax.rem(j, 2)
        arrive(j, slot)

        @pl.when(j > 0)
        def _():
            fetch(j - 1, 1 - slot)

        z = jnp.dot(q_bd, k_buf[slot].astype(BF16), preferred_element_type=F32)
        a, mass = _sb_weights(z, upper2, mass, None)
        acc_ref[...] += _dot_nt(a, v_buf[slot].astype(BF16))
        return j - 1, mass, jnp.min(mass)

    j_end, _, _ = lax.while_loop(cond, body, (last, mass, jnp.min(mass)))

    @pl.when(j_end >= 0)
    def _():
        arrive(j_end, lax.rem(j_end, 2))

    out_head = lax.broadcasted_iota(jnp.int32, (t_new, SB_WIDTH), 1) // HEAD_DIM
    out = jnp.zeros((t_new, SB_WIDTH), F32)
    for h in range(N_SB_HEADS):
        out = out + jnp.where(out_head == h, acc_ref[h * t_new:(h + 1) * t_new, :], 0.0)
    o_ref[...] = out.astype(o_ref.dtype)


def _sb_decode(q, k_new, v_new, cache_kt, cache_vt, *, batch, t_new):
    past = cache_kt.shape[2]
    tk = SB_TK
    assert past % tk == 0
    pad = ((0, 0), (0, LANES - t_new), (0, 0))
    kn = jnp.pad(k_new.reshape(batch, t_new, SB_WIDTH), pad)
    vn = jnp.pad(v_new.reshape(batch, t_new, SB_WIDTH), pad)
    row_spec = pl.BlockSpec((None, t_new, SB_WIDTH), lambda b: (b, 0, 0))
    new_spec = pl.BlockSpec((None, LANES, SB_WIDTH), lambda b: (b, 0, 0))
    hbm_spec = pl.BlockSpec(memory_space=pl.ANY)
    out = pl.pallas_call(
        functools.partial(_sb_decode_kernel, t_new=t_new, tk=tk, n_blocks=past // tk),
        out_shape=jax.ShapeDtypeStruct((batch, t_new, SB_WIDTH), BF16),
        grid=(batch,),
        in_specs=[row_spec, new_spec, new_spec, hbm_spec, hbm_spec],
        out_specs=row_spec,
        scratch_shapes=[pltpu.VMEM((2, SB_WIDTH, tk), F32),
                        pltpu.VMEM((2, SB_WIDTH, tk), F32),
                        pltpu.SemaphoreType.DMA((2, 2)),
                        pltpu.VMEM((N_SB_HEADS * t_new, SB_WIDTH), F32)],
        compiler_params=_cparams("arbitrary"),
        name="sb_decode",
    )(q.reshape(batch, t_new, SB_WIDTH), kn, vn, cache_kt, cache_vt)
    return out.reshape(batch * t_new, SB_WIDTH)


def _heads_major(x):
    b, t, h, d = x.shape
    return jnp.transpose(x, (0, 2, 3, 1)).reshape(b, h * d, t)


def _heads_minor(xt, heads):
    b, hd, t = xt.shape
    return jnp.transpose(xt.reshape(b, heads, hd // heads, t), (0, 3, 1, 2))


def _sgu_kernel(u_ref, v_ref, w_ref, b_ref, o_ref, *, span, n_chunks):
    lane = lax.broadcasted_iota(jnp.int32, (1, LANES), 1)
    group0 = lane < HEAD_DIM
    t = lax.broadcasted_iota(jnp.int32, (span, span), 0)
    s = lax.broadcasted_iota(jnp.int32, (span, span), 1)
    tril = s <= t
    for p in range(SGU_WIDTH // LANES):
        w_pair = jnp.concatenate(
            [jnp.where(tril, w_ref[2 * p], 0.0), jnp.where(tril, w_ref[2 * p + 1], 0.0)],
            axis=1).astype(BF16)
        cols = slice(p * LANES, (p + 1) * LANES)
        for c in range(n_chunks):
            rows = slice(c * span, (c + 1) * span)
            vv = v_ref[rows, cols]
            zv = jnp.zeros_like(vv)
            v_pair = jnp.concatenate([jnp.where(group0, vv, zv),
                                      jnp.where(group0, zv, vv)], axis=0)
            mixed = jnp.dot(w_pair, v_pair, preferred_element_type=F32) + b_ref[:, cols]
            o_ref[rows, cols] = (u_ref[rows, cols] * mixed).astype(o_ref.dtype)


def _sgu(u, v, w, bias, *, tm, name):
    m = u.shape[0]
    span = w.shape[-1]
    blk = pl.BlockSpec((tm, SGU_WIDTH), lambda i: (i, 0))
    return pl.pallas_call(
        functools.partial(_sgu_kernel, span=span, n_chunks=tm // span),
        out_shape=jax.ShapeDtypeStruct((m, SGU_WIDTH), BF16),
        grid=(m // tm,),
        in_specs=[blk, blk, _const_spec(w.shape), _const_spec(bias.shape)],
        out_specs=blk, compiler_params=_cparams("arbitrary"), name=name,
    )(u, v, w, bias)


def _mix_out_kernel(y_ref, o_ref, qm_ref, mk_ref, mv_ref, w_ref, out_ref):
    lane = lax.broadcasted_iota(jnp.int32, (1, MEM_WIDTH), 1) // HEAD_DIM
    sublane = lax.broadcasted_iota(jnp.int32, (MEM_WIDTH, 1), 0) // HEAD_DIM
    qm = qm_ref[...] * jnp.asarray(SB_SCALE, BF16)
    zq = jnp.zeros_like(qm)
    mk = mk_ref[...].astype(BF16)
    mv = mv_ref[...].astype(BF16)
    zv = jnp.zeros_like(mv)
    scores = [jnp.dot(jnp.where(lane == h, qm, zq), mk, preferred_element_type=F32)
              for h in range(N_MEM_HEADS)]
    width = o_ref.shape[1]
    o = o_ref[...]
    chunks = [slice(c * OUT_CHUNK, (c + 1) * OUT_CHUNK) for c in range(D_MODEL // OUT_CHUNK)]
    mixer_part = [jnp.dot(o, w_ref[:width, cols], preferred_element_type=F32)
                  for cols in chunks]
    mem = 0.0
    for h, s in enumerate(scores):
        e = jnp.exp(s - jnp.max(s, axis=-1, keepdims=True))
        p = e / jnp.sum(e, axis=-1, keepdims=True)
        mem = mem + _dot_nt(p.astype(BF16), jnp.where(sublane == h, mv, zv))
    mem = mem.astype(BF16)
    for cols, part in zip(chunks, mixer_part):
        out_ref[:, cols] = (y_ref[:, cols] + part) + jnp.dot(
            mem, w_ref[width:, cols], preferred_element_type=F32)


def _mix_out(y, o, qm, mem_kt, mem_vt, w_out, layer, *, tm, rows_per_batch, name):
    m = y.shape[0]
    assert rows_per_batch % tm == 0
    per = rows_per_batch // tm
    mem_spec = pl.BlockSpec((None, MEM_WIDTH, N_MEM), lambda i: (i // per, 0, 0))
    return pl.pallas_call(
        _mix_out_kernel,
        out_shape=jax.ShapeDtypeStruct((m, D_MODEL), F32),
        grid=(m // tm,),
        in_specs=[pl.BlockSpec((tm, D_MODEL), lambda i: (i, 0)),
                  pl.BlockSpec((tm, SB_WIDTH), lambda i: (i, 0)),
                  pl.BlockSpec((tm, MEM_WIDTH), lambda i: (i, 0)),
                  mem_spec, mem_spec, _layer_spec(w_out, layer)],
        out_specs=pl.BlockSpec((tm, D_MODEL), lambda i: (i, 0)),
        compiler_params=_cparams("arbitrary"), name=name,
    )(y, o, qm, mem_kt, mem_vt, w_out)


def _ffn_kernel(y_ref, g_ref, wg_ref, wu_ref, wd_ref, *rest, final):
    if final:
        gf_ref, out_ref = rest
    else:
        (out_ref,) = rest
    y = y_ref[...]
    h = _rmsnorm_rows(y, g_ref[...]).astype(BF16)
    acc = y
    for c in range(D_FF // FF_CHUNK):
        cols = slice(c * FF_CHUNK, (c + 1) * FF_CHUNK)
        gate = jnp.dot(h, wg_ref[:, cols], preferred_element_type=F32)
        up = jnp.dot(h, wu_ref[:, cols], preferred_element_type=F32)
        act = (jax.nn.silu(gate) * up).astype(BF16)
        acc = acc + jnp.dot(act, wd_ref[cols, :], preferred_element_type=F32)
    if final:
        acc = _rmsnorm_rows(acc, gf_ref[...])
    out_ref[...] = acc


def _ffn(y, g, w_gate, w_up, w_down, layer, *, tm, g_final=None, name):
    m = y.shape[0]
    row = pl.BlockSpec((tm, D_MODEL), lambda i: (i, 0))
    args = [y, g, w_gate, w_up, w_down]
    in_specs = [row] + [_layer_spec(a, layer) for a in args[1:]]
    if g_final is not None:
        in_specs.append(_const_spec((1, D_MODEL)))
        args.append(g_final.reshape(1, D_MODEL))
    return pl.pallas_call(
        functools.partial(_ffn_kernel, final=g_final is not None),
        out_shape=jax.ShapeDtypeStruct((m, D_MODEL), F32),
        grid=(m // tm,), in_specs=in_specs, out_specs=row,
        compiler_params=_cparams("arbitrary"), name=name,
    )(*args)


_PLAN_A_PROMPT = ((0, SB_WIDTH, "bf16"), (SB_WIDTH, 2 * SB_WIDTH, "f32T+bf16"),
                  (2 * SB_WIDTH, 3 * SB_WIDTH, "f32T+bf16"),
                  (3 * SB_WIDTH, 3 * SB_WIDTH + MEM_WIDTH, "bf16"))
_PLAN_A_SAMPLE = ((0, SB_WIDTH, "f32"), (SB_WIDTH, 2 * SB_WIDTH, "f32+bf16"),
                  (2 * SB_WIDTH, 3 * SB_WIDTH, "f32+bf16"),
                  (3 * SB_WIDTH, 3 * SB_WIDTH + MEM_WIDTH, "bf16"))
_PLAN_B_PROMPT = ((SGU_WIDTH, 2 * SGU_WIDTH, "gelu_norm_b"), (0, SGU_WIDTH, "gelu"),
                  (2 * SGU_WIDTH, 2 * SGU_WIDTH + MEM_WIDTH, "bf16"))
_PLAN_B_SAMPLE = ((SGU_WIDTH, 2 * SGU_WIDTH, "gelu_norm"),) + _PLAN_B_PROMPT[1:]
_PLAN_MEM = ((0, MEM_WIDTH, "f32T"), (MEM_WIDTH, 2 * MEM_WIDTH, "f32T"))


def kernel(x_prompt, x_sample, cache_sb_k, cache_sb_v, cache_mem_k, cache_mem_v,
           mem_prompt, g_mix, w_in_a, w_in_b, w_sp, b_sp, g_sgu, g_mem, w_mem_kv,
           w_out, g_ffn, w_gate, w_up, w_down, g_final):
    batch, seq, _ = x_prompt.shape
    dec_batch, dec_seq, _ = x_sample.shape
    depth = g_mix.shape[0]
    tm_p = 512
    tm_s = dec_batch * dec_seq

    y_p = x_prompt.reshape(batch * seq, D_MODEL)
    y_s = x_sample.reshape(tm_s, D_MODEL)
    mem_rows = mem_prompt.reshape(batch * N_MEM, D_MODEL)

    g_mix, g_mem, g_ffn, g_sgu = (_as_rows(g) for g in (g_mix, g_mem, g_ffn, g_sgu))
    w_in_a, w_in_b, w_mem_kv, w_out, w_gate, w_up, w_down = (
        w.astype(BF16) for w in (w_in_a, w_in_b, w_mem_kv, w_out, w_gate, w_up, w_down))

    sb_k_p, sb_v_p, sb_k_s, sb_v_s, mem_k_p, mem_v_p, sgu_v_s = ([] for _ in range(7))
    for l in range(depth):
        mkt_p, mvt_p = _norm_proj(mem_rows, (g_mem, l), (w_mem_kv, l), _PLAN_MEM,
                                  tm=N_MEM, rows_per_batch=N_MEM, name=f"mem_kv{l}")
        mem_k_p.append(_heads_minor(mkt_p, N_MEM_HEADS))
        mem_v_p.append(_heads_minor(mvt_p, N_MEM_HEADS))
        mkt_s, mvt_s = _heads_major(cache_mem_k[l]), _heads_major(cache_mem_v[l])

        if l % 2 == 0:
            ia = l // 2
            q, kt_f, k_b, vt_f, v_b, qm_p = _norm_proj(
                y_p, (g_mix, l), (w_in_a, ia), _PLAN_A_PROMPT, tm=tm_p,
                rows_per_batch=seq, name=f"in_a{l}_p")
            sb_k_p.append(_heads_minor(kt_f, N_SB_HEADS))
            sb_v_p.append(_heads_minor(vt_f, N_SB_HEADS))
            o_p = _sb_prompt(q, k_b, v_b, batch=batch, seq=seq)

            q, k_f, k_b, v_f, v_b, qm_s = _norm_proj(
                y_s, (g_mix, l), (w_in_a, ia), _PLAN_A_SAMPLE, tm=tm_s, name=f"in_a{l}_s")
            sb_k_s.append(k_f.reshape(dec_batch, dec_seq, N_SB_HEADS, HEAD_DIM))
            sb_v_s.append(v_f.reshape(dec_batch, dec_seq, N_SB_HEADS, HEAD_DIM))
            o_s = _sb_decode(q, k_b, v_b, _heads_major(cache_sb_k[ia]),
                             _heads_major(cache_sb_v[ia]),
                             batch=dec_batch, t_new=dec_seq)
        else:
            ib = l // 2
            v_b, u, qm_p = _norm_proj(
                y_p, (g_mix, l), (w_in_b, ib), _PLAN_B_PROMPT, tm=tm_p,
                gate_norm=(g_sgu, ib), name=f"in_b{l}_p")
            span = min(seq, MLP_CHUNK)
            bias = jnp.repeat(b_sp[ib][:, :span].T, HEAD_DIM, axis=1)
            o_p = _sgu(u, v_b, w_sp[ib][:, :span, :span], bias, tm=tm_p, name=f"sgu{l}_p")

            v_f, v_b, u, qm_s = _norm_proj(
                y_s, (g_mix, l), (w_in_b, ib), _PLAN_B_SAMPLE, tm=tm_s,
                gate_norm=(g_sgu, ib), name=f"in_b{l}_s")
            sgu_v_s.append(v_f.reshape(dec_batch, dec_seq, SGU_WIDTH))
            span = min(dec_seq, MLP_CHUNK)
            reps = MLP_CHUNK // span
            eye = jnp.eye(reps, dtype=F32)
            w_bd = jnp.einsum("ab,gts->gatbs", eye, w_sp[ib][:, :span, :span]).reshape(
                -1, MLP_CHUNK, MLP_CHUNK)
            bias = jnp.tile(jnp.repeat(b_sp[ib][:, :span].T, HEAD_DIM, axis=1), (reps, 1))
            o_s = _sgu(u, v_b, w_bd, bias, tm=tm_s, name=f"sgu{l}_s")

        y_p = _mix_out(y_p, o_p, qm_p, mkt_p, mvt_p, w_out, l, tm=tm_p,
                       rows_per_batch=seq, name=f"mix_out{l}_p")
        y_s = _mix_out(y_s, o_s, qm_s, mkt_s, mvt_s, w_out, l, tm=dec_seq,
                       rows_per_batch=dec_seq, name=f"mix_out{l}_s")
        last = g_final if l == depth - 1 else None
        y_p = _ffn(y_p, g_ffn, w_gate, w_up, w_down, l, tm=tm_p,
                   g_final=last, name=f"ffn{l}_p")
        y_s = _ffn(y_s, g_ffn, w_gate, w_up, w_down, l, tm=tm_s,
                   g_final=last, name=f"ffn{l}_s")

    return (y_p.reshape(batch, seq, D_MODEL), y_s.reshape(dec_batch, dec_seq, D_MODEL),
            jnp.stack(sb_k_p), jnp.stack(sb_v_p), jnp.stack(sb_k_s), jnp.stack(sb_v_s),
            jnp.stack(mem_k_p), jnp.stack(mem_v_p), jnp.stack(sgu_v_s))
```

```python
import functools

import jax
import jax.numpy as jnp
from jax import lax
from jax.experimental import pallas as pl
from jax.experimental.pallas import tpu as pltpu

F32 = jnp.float32
BF16 = jnp.bfloat16

D_MODEL = 1024
HEAD_DIM = 64
SB_WIDTH = 768
N_SB_HEADS = SB_WIDTH // HEAD_DIM
SGU_WIDTH = 768
MLP_CHUNK = 128
MEM_WIDTH = 256
N_MEM = 256
N_MEM_HEADS = MEM_WIDTH // HEAD_DIM
D_FF = 2816
EPS = 1e-6
SB_SCALE = HEAD_DIM ** -0.5
LOG2E = 1.4426950408889634

LANES = 128
V7X_VMEM_BYTES = 64 * 1024 * 1024
VMEM_LIMIT = V7X_VMEM_BYTES - 8 * 1024 * 1024

SB_DEAD_MASS = 106.0

SB_TQ = 256
SB_TK = 256
SB_PAIRS = 6
SB_EARLY = 176
FF_CHUNK = 256
OUT_CHUNK = 256
PROJ_CHUNK = 256


def _cparams(*semantics):
    return pltpu.CompilerParams(dimension_semantics=semantics,
                                vmem_limit_bytes=VMEM_LIMIT)


def _const_spec(shape):
    zeros = (0,) * len(shape)
    return pl.BlockSpec(shape, lambda *_: zeros, pipeline_mode=pl.Buffered(1))


def _layer_spec(stacked, layer):
    return pl.BlockSpec((None,) + stacked.shape[1:], lambda *_: (layer, 0, 0),
                        pipeline_mode=pl.Buffered(1))


def _as_rows(g):
    return g.reshape(g.shape[0], 1, g.shape[1])


def _rmsnorm_rows(x, g):
    ms = jnp.mean(x * x, axis=-1, keepdims=True)
    return (x * lax.rsqrt(ms + EPS)) * g


def _dot_nt(a, b):
    return lax.dot_general(a, b, (((1,), (1,)), ((), ())), preferred_element_type=F32)


_PLAN_FORMS = {
    "bf16": ("bf16",),
    "f32": ("f32",),
    "f32+bf16": ("f32", "bf16"),
    "f32T": ("f32T",),
    "f32T+bf16": ("f32T", "bf16"),
    "gelu": ("f32",),
    "gelu_norm": ("f32", "bf16"),
    "gelu_norm_b": ("bf16",),
}


def _norm_proj_kernel(x_ref, g_ref, w_ref, *rest, plan, has_gate_norm):
    if has_gate_norm:
        gn_ref, out_refs = rest[0], rest[1:]
    else:
        gn_ref, out_refs = None, rest
    h = _rmsnorm_rows(x_ref[...], g_ref[...]).astype(BF16)
    tasks, k = [], 0
    for lo, hi, kind in plan:
        refs = out_refs[k:k + len(_PLAN_FORMS[kind])]
        k += len(refs)
        tasks += [(lo, hi, kind, refs, c, min(c + PROJ_CHUNK, hi))
                  for c in range(lo, hi, PROJ_CHUNK)]

    def product(task):
        a, b = task[4:]
        return jnp.dot(h, w_ref[:, a:b], preferred_element_type=F32)

    def store(refs, forms, a, b, value):
        for ref, form in zip(refs, forms):
            if form == "f32T":
                ref[a:b, :] = value.T
            else:
                ref[:, a:b] = value.astype(BF16 if form == "bf16" else F32)

    acts = []
    ahead = product(tasks[0])
    for n, task in enumerate(tasks):
        lo, hi, kind, refs, a, b = task
        p = ahead
        if n + 1 < len(tasks):
            ahead = product(tasks[n + 1])
        if kind.startswith("gelu"):
            p = jax.nn.gelu(p)
        if not kind.startswith("gelu_norm"):
            store(refs, _PLAN_FORMS[kind], a - lo, b - lo, p)
            continue
        acts.append((a - lo, b - lo, p))
        if b == hi:
            ms = sum(jnp.sum(t * t, axis=-1, keepdims=True) for _, _, t in acts) / (hi - lo)
            inv = lax.rsqrt(ms + EPS)
            for a0, b0, t in acts:
                store(refs, _PLAN_FORMS[kind], a0, b0, (t * inv) * gn_ref[:, a0:b0])
            acts = []


def _norm_proj(x, g, w, plan, *, tm, rows_per_batch=None, gate_norm=None, name):
    m, d = x.shape
    out_shape, out_specs = [], []
    for lo, hi, kind in plan:
        for form in _PLAN_FORMS[kind]:
            if form == "f32T":
                per = rows_per_batch // tm
                out_shape.append(jax.ShapeDtypeStruct(
                    (m // rows_per_batch, hi - lo, rows_per_batch), F32))
                out_specs.append(pl.BlockSpec(
                    (None, hi - lo, tm), lambda i, per=per: (i // per, 0, i % per)))
            else:
                out_shape.append(jax.ShapeDtypeStruct(
                    (m, hi - lo), BF16 if form == "bf16" else F32))
                out_specs.append(pl.BlockSpec((tm, hi - lo), lambda i: (i, 0)))
    layered = [g, w] + ([gate_norm] if gate_norm is not None else [])
    in_specs = [pl.BlockSpec((tm, d), lambda i: (i, 0))] + [
        _layer_spec(arr, layer) for arr, layer in layered]
    args = [x] + [arr for arr, _ in layered]
    return pl.pallas_call(
        functools.partial(_norm_proj_kernel, plan=plan,
                          has_gate_norm=gate_norm is not None),
        out_shape=out_shape, grid=(m // tm,), in_specs=in_specs,
        out_specs=out_specs, compiler_params=_cparams("arbitrary"), name=name,
    )(*args)


def _upper2(n):
    j = lax.broadcasted_iota(jnp.int32, (2 * n, n), 0) % n
    s = lax.broadcasted_iota(jnp.int32, (2 * n, n), 1)
    return jnp.where(j >= s, 1.0, 0.0).astype(BF16)


def _sb_weights(z, upper2, mass, causal):
    soft = jnp.maximum(z, 0.0) + jnp.log(1.0 + jnp.exp2(jnp.abs(z) * -LOG2E))
    if causal is not None:
        soft = jnp.where(causal, soft, 0.0)
    hi = soft.astype(BF16)
    lo = (soft - hi.astype(F32)).astype(BF16)
    from_here = jnp.dot(jnp.concatenate([hi, lo], axis=1), upper2,
                        preferred_element_type=F32) + mass
    a = jnp.exp2((z - from_here) * LOG2E)
    if causal is not None:
        a = jnp.where(causal, a, 0.0)
    return a.astype(BF16), mass + jnp.sum(soft, axis=-1, keepdims=True)


def _sb_prompt_kernel(q_ref, k_ref, v_ref, o_ref, acc_ref, mass_ref, *,
                      tq, tk, pairs, early):
    i = pl.program_id(2)
    lane = lax.broadcasted_iota(jnp.int32, (1, LANES), 1)
    head0 = lane < HEAD_DIM
    upper2 = _upper2(tk)
    row = lax.broadcasted_iota(jnp.int32, (2 * tq, tk), 0) % tq
    col = lax.broadcasted_iota(jnp.int32, (2 * tq, tk), 1)
    has_prev = i > 0
    j_prev = jnp.maximum(i - 1, 0)

    def both_heads(x, lo, hi):
        return jnp.concatenate([x[lo:hi], x[tq + lo:tq + hi]], axis=0)

    def sweeper(p):
        lanes = slice(p * LANES, (p + 1) * LANES)
        q = q_ref[:, lanes] * jnp.asarray(SB_SCALE, BF16)
        zq = jnp.zeros_like(q)
        q2 = jnp.concatenate([jnp.where(head0, q, zq), jnp.where(head0, zq, q)], axis=0)

        def logits(qs, j):
            return _dot_nt(qs, k_ref[pl.ds(pl.multiple_of(j * tk, tk), tk), lanes])

        def values(a, j):
            n = a.shape[0] // 2
            vb = v_ref[pl.ds(pl.multiple_of(j * tk, tk), tk), lanes]
            zv = jnp.zeros_like(vb)
            return (jnp.dot(a[:n], jnp.where(head0, vb, zv), preferred_element_type=F32)
                    + jnp.dot(a[n:], jnp.where(head0, zv, vb), preferred_element_type=F32))

        return q2, logits, values

    late = tq - early
    least_early = least_late = jnp.asarray(jnp.inf, F32)
    for p in range(pairs):
        q2, logits, values = sweeper(p)
        z_diag, z_prev = logits(q2, i), logits(both_heads(q2, 0, early), j_prev)
        a_diag, mass_diag = _sb_weights(z_diag, upper2, jnp.zeros((2 * tq, 1), F32),
                                        col < row)
        mass_e = both_heads(mass_diag, 0, early)
        a_prev, mass_prev = _sb_weights(z_prev, upper2, mass_e, None)
        acc = values(a_diag, i)
        acc_early = acc[:early] + jnp.where(has_prev, values(a_prev, j_prev), 0.0)
        acc_ref[:, p * LANES:(p + 1) * LANES] = jnp.concatenate(
            [acc_early, acc[early:]], axis=0)
        mass_e = jnp.where(has_prev, mass_prev, mass_e)
        mass_l = both_heads(mass_diag, early, tq)
        mass_ref[p] = jnp.concatenate(
            [mass_e[:early], mass_l[:late], mass_e[early:], mass_l[late:]], axis=0)
        least_early = jnp.minimum(least_early, jnp.min(mass_e))
        least_late = jnp.minimum(least_late, jnp.min(mass_l))

    late_row = lax.broadcasted_iota(jnp.int32, (2 * tq, 1), 0) % tq >= early

    def cond(state):
        j, least = state
        return jnp.logical_and(j >= 0, least < SB_DEAD_MASS)

    def body(state):
        j, _ = state
        fresh = jnp.logical_or(late_row, j != i - 1)
        least = jnp.asarray(jnp.inf, F32)
        for p in range(pairs):
            q2, logits, values = sweeper(p)
            mass_old = mass_ref[p]
            a, mass = _sb_weights(logits(q2, j), upper2, mass_old, None)
            a = jnp.where(fresh, a, jnp.zeros_like(a))
            mass = jnp.where(fresh, mass, mass_old)
            acc_ref[:, p * LANES:(p + 1) * LANES] += values(a, j)
            mass_ref[p] = mass
            least = jnp.minimum(least, jnp.min(mass))
        return j - 1, least

    j_start = jnp.where(least_late < SB_DEAD_MASS, i - 1, i - 2)
    lax.while_loop(cond, body, (j_start, jnp.minimum(least_early, least_late)))
    o_ref[...] = acc_ref[...].astype(o_ref.dtype)


def _sb_prompt(q, k, v, *, batch, seq):
    tq, tk, pairs = SB_TQ, SB_TK, SB_PAIRS
    assert tq == tk and seq % tq == 0
    q3, k3, v3 = (a.reshape(batch, seq, SB_WIDTH) for a in (q, k, v))
    width = pairs * LANES
    q_spec = pl.BlockSpec((None, tq, width), lambda b, p, i: (b, i, p))
    kv_spec = pl.BlockSpec((None, seq, width), lambda b, p, i: (b, 0, p))
    out = pl.pallas_call(
        functools.partial(_sb_prompt_kernel, tq=tq, tk=tk, pairs=pairs, early=SB_EARLY),
        out_shape=jax.ShapeDtypeStruct((batch, seq, SB_WIDTH), BF16),
        grid=(batch, SB_WIDTH // width, seq // tq),
        in_specs=[q_spec, kv_spec, kv_spec], out_specs=q_spec,
        scratch_shapes=[pltpu.VMEM((tq, width), F32),
                        pltpu.VMEM((pairs, 2 * tq, 1), F32)],
        compiler_params=_cparams("arbitrary", "arbitrary", "arbitrary"),
        name="sb_prompt",
    )(q3, k3, v3)
    return out.reshape(batch * seq, SB_WIDTH)


def _sb_decode_kernel(q_ref, kn_ref, vn_ref, kc_hbm, vc_hbm, o_ref,
                      k_buf, v_buf, sems, acc_ref, *, t_new, tk, n_blocks):
    b = pl.program_id(0)
    n_rows = N_SB_HEADS * t_new

    def block_copies(j, slot):
        keys = pl.ds(pl.multiple_of(j * tk, tk), tk)
        return (pltpu.make_async_copy(kc_hbm.at[b, :, keys], k_buf.at[slot], sems.at[0, slot]),
                pltpu.make_async_copy(vc_hbm.at[b, :, keys], v_buf.at[slot], sems.at[1, slot]))

    def fetch(j, slot):
        for copy in block_copies(j, slot):
            copy.start()

    def arrive(j, slot):
        for copy in block_copies(j, slot):
            copy.wait()

    last = n_blocks - 1
    fetch(last, last % 2)

    row_head = lax.broadcasted_iota(jnp.int32, (n_rows, SB_WIDTH), 0) // t_new
    col_head = lax.broadcasted_iota(jnp.int32, (n_rows, SB_WIDTH), 1) // HEAD_DIM
    q_rows = jnp.concatenate([q_ref[...]] * N_SB_HEADS, axis=0)
    q_bd = jnp.where(row_head == col_head, q_rows * SB_SCALE, 0.0).astype(BF16)

    n_pad = kn_ref.shape[0]
    q_idx = lax.broadcasted_iota(jnp.int32, (n_rows, n_pad), 0) % t_new
    s_idx = lax.broadcasted_iota(jnp.int32, (n_rows, n_pad), 1)
    a, mass = _sb_weights(_dot_nt(q_bd, kn_ref[...]), _upper2(n_pad),
                          jnp.zeros((n_rows, 1), F32), s_idx < q_idx)
    acc_ref[...] = jnp.dot(a, vn_ref[...], preferred_element_type=F32)
    upper2 = _upper2(tk)

    def cond(state):
        j, _, least = state
        return jnp.logical_and(j >= 0, least < SB_DEAD_MASS)

    def body(state):
        j, mass, _ = state
        slot = lax.rem(j, 2)
        arrive(j, slot)

        @pl.when(j > 0)
        def _():
            fetch(j - 1, 1 - slot)

        z = jnp.dot(q_bd, k_buf[slot].astype(BF16), preferred_element_type=F32)
        a, mass = _sb_weights(z, upper2, mass, None)
        acc_ref[...] += _dot_nt(a, v_buf[slot].astype(BF16))
        return j - 1, mass, jnp.min(mass)

    j_end, _, _ = lax.while_loop(cond, body, (last, mass, jnp.min(mass)))

    @pl.when(j_end >= 0)
    def _():
        arrive(j_end, lax.rem(j_end, 2))

    out_head = lax.broadcasted_iota(jnp.int32, (t_new, SB_WIDTH), 1) // HEAD_DIM
    out = jnp.zeros((t_new, SB_WIDTH), F32)
    for h in range(N_SB_HEADS):
        out = out + jnp.where(out_head == h, acc_ref[h * t_new:(h + 1) * t_new, :], 0.0)
    o_ref[...] = out.astype(o_ref.dtype)


def _sb_decode(q, k_new, v_new, cache_kt, cache_vt, *, batch, t_new):
    past = cache_kt.shape[2]
    tk = SB_TK
    assert past % tk == 0
    pad = ((0, 0), (0, LANES - t_new), (0, 0))
    kn = jnp.pad(k_new.reshape(batch, t_new, SB_WIDTH), pad)
    vn = jnp.pad(v_new.reshape(batch, t_new, SB_WIDTH), pad)
    row_spec = pl.BlockSpec((None, t_new, SB_WIDTH), lambda b: (b, 0, 0))
    new_spec = pl.BlockSpec((None, LANES, SB_WIDTH), lambda b: (b, 0, 0))
    hbm_spec = pl.BlockSpec(memory_space=pl.ANY)
    out = pl.pallas_call(
        functools.partial(_sb_decode_kernel, t_new=t_new, tk=tk, n_blocks=past // tk),
        out_shape=jax.ShapeDtypeStruct((batch, t_new, SB_WIDTH), BF16),
        grid=(batch,),
        in_specs=[row_spec, new_spec, new_spec, hbm_spec, hbm_spec],
        out_specs=row_spec,
        scratch_shapes=[pltpu.VMEM((2, SB_WIDTH, tk), F32),
                        pltpu.VMEM((2, SB_WIDTH, tk), F32),
                        pltpu.SemaphoreType.DMA((2, 2)),
                        pltpu.VMEM((N_SB_HEADS * t_new, SB_WIDTH), F32)],
        compiler_params=_cparams("arbitrary"),
        name="sb_decode",
    )(q.reshape(batch, t_new, SB_WIDTH), kn, vn, cache_kt, cache_vt)
    return out.reshape(batch * t_new, SB_WIDTH)


def _heads_major(x):
    b, t, h, d = x.shape
    return jnp.transpose(x, (0, 2, 3, 1)).reshape(b, h * d, t)


def _heads_minor(xt, heads):
    b, hd, t = xt.shape
    return jnp.transpose(xt.reshape(b, heads, hd // heads, t), (0, 3, 1, 2))


def _sgu_kernel(u_ref, v_ref, w_ref, b_ref, o_ref, *, span, n_chunks):
    lane = lax.broadcasted_iota(jnp.int32, (1, LANES), 1)
    group0 = lane < HEAD_DIM
    t = lax.broadcasted_iota(jnp.int32, (span, span), 0)
    s = lax.broadcasted_iota(jnp.int32, (span, span), 1)
    tril = s <= t
    for p in range(SGU_WIDTH // LANES):
        w_pair = jnp.concatenate(
            [jnp.where(tril, w_ref[2 * p], 0.0), jnp.where(tril, w_ref[2 * p + 1], 0.0)],
            axis=1).astype(BF16)
        cols = slice(p * LANES, (p + 1) * LANES)
        for c in range(n_chunks):
            rows = slice(c * span, (c + 1) * span)
            vv = v_ref[rows, cols]
            zv = jnp.zeros_like(vv)
            v_pair = jnp.concatenate([jnp.where(group0, vv, zv),
                                      jnp.where(group0, zv, vv)], axis=0)
            mixed = jnp.dot(w_pair, v_pair, preferred_element_type=F32) + b_ref[:, cols]
            o_ref[rows, cols] = (u_ref[rows, cols] * mixed).astype(o_ref.dtype)


def _sgu(u, v, w, bias, *, tm, name):
    m = u.shape[0]
    span = w.shape[-1]
    blk = pl.BlockSpec((tm, SGU_WIDTH), lambda i: (i, 0))
    return pl.pallas_call(
        functools.partial(_sgu_kernel, span=span, n_chunks=tm // span),
        out_shape=jax.ShapeDtypeStruct((m, SGU_WIDTH), BF16),
        grid=(m // tm,),
        in_specs=[blk, blk, _const_spec(w.shape), _const_spec(bias.shape)],
        out_specs=blk, compiler_params=_cparams("arbitrary"), name=name,
    )(u, v, w, bias)


def _mix_rows(y_ref, o_ref, qm_ref, mk_ref, mv_ref, w_ref):
    lane = lax.broadcasted_iota(jnp.int32, (1, MEM_WIDTH), 1) // HEAD_DIM
    sublane = lax.broadcasted_iota(jnp.int32, (MEM_WIDTH, 1), 0) // HEAD_DIM
    qm = qm_ref[...] * jnp.asarray(SB_SCALE, BF16)
    zq = jnp.zeros_like(qm)
    mk = mk_ref[...].astype(BF16)
    mv = mv_ref[...].astype(BF16)
    zv = jnp.zeros_like(mv)
    scores = [jnp.dot(jnp.where(lane == h, qm, zq), mk, preferred_element_type=F32)
              for h in range(N_MEM_HEADS)]
    width = o_ref.shape[1]
    o = o_ref[...]
    chunks = [slice(c * OUT_CHUNK, (c + 1) * OUT_CHUNK) for c in range(D_MODEL // OUT_CHUNK)]
    mixer_part = [jnp.dot(o, w_ref[:width, cols], preferred_element_type=F32)
                  for cols in chunks]
    mem = 0.0
    for h, s in enumerate(scores):
        e = jnp.exp(s - jnp.max(s, axis=-1, keepdims=True))
        p = e / jnp.sum(e, axis=-1, keepdims=True)
        mem = mem + _dot_nt(p.astype(BF16), jnp.where(sublane == h, mv, zv))
    mem = mem.astype(BF16)
    return jnp.concatenate(
        [(y_ref[:, cols] + part) + jnp.dot(mem, w_ref[width:, cols],
                                           preferred_element_type=F32)
         for cols, part in zip(chunks, mixer_part)], axis=1)


def _gate_mix_kernel(y_ref, g_ref, w_ref, gn_ref, wsp_ref, b_ref, mk_ref, mv_ref,
                     wout_ref, out_ref, o_scr, *, span):
    h = _rmsnorm_rows(y_ref[...], g_ref[...]).astype(BF16)

    def proj(lo, hi):
        return [jnp.dot(h, w_ref[:, a:min(a + PROJ_CHUNK, hi)], preferred_element_type=F32)
                for a in range(lo, hi, PROJ_CHUNK)]

    acts = [jax.nn.gelu(p) for p in proj(SGU_WIDTH, 2 * SGU_WIDTH)]
    ms = sum(jnp.sum(t * t, axis=-1, keepdims=True) for t in acts) / SGU_WIDTH
    v = (jnp.concatenate(acts, axis=1) * lax.rsqrt(ms + EPS)) * gn_ref[...]
    u = jnp.concatenate([jax.nn.gelu(p) for p in proj(0, SGU_WIDTH)], axis=1)
    (qm,) = proj(2 * SGU_WIDTH, 2 * SGU_WIDTH + MEM_WIDTH)
    _sgu_kernel(u, v.astype(BF16), wsp_ref, b_ref, o_scr, span=span,
                n_chunks=y_ref.shape[0] // span)
    out_ref[...] = _mix_rows(y_ref, o_scr, qm.astype(BF16), mk_ref, mv_ref, wout_ref)


def _gate_mix(y, g, w_in, gate_norm, w_sp, bias, mem_kt, mem_vt, w_out, *,
              tm, rows_per_batch, name):
    m = y.shape[0]
    span = w_sp.shape[-1]
    assert rows_per_batch % tm == 0 and tm % span == 0
    per = rows_per_batch // tm
    row = pl.BlockSpec((tm, D_MODEL), lambda i: (i, 0))
    mem_spec = pl.BlockSpec((None, MEM_WIDTH, N_MEM), lambda i: (i // per, 0, 0))
    layered = [g, w_in, gate_norm]
    return pl.pallas_call(
        functools.partial(_gate_mix_kernel, span=span),
        out_shape=jax.ShapeDtypeStruct((m, D_MODEL), F32),
        grid=(m // tm,),
        in_specs=[row] + [_layer_spec(*pair) for pair in layered] + [
            _const_spec(w_sp.shape), _const_spec(bias.shape), mem_spec, mem_spec,
            _layer_spec(*w_out)],
        out_specs=row,
        scratch_shapes=[pltpu.VMEM((tm, SGU_WIDTH), BF16)],
        compiler_params=_cparams("arbitrary"), name=name,
    )(y, *[arr for arr, _ in layered], w_sp, bias, mem_kt, mem_vt, w_out[0])


def _ffn_rows(y, g_ref, wg_ref, wu_ref, wd_ref, gf_ref):
    h = _rmsnorm_rows(y, g_ref[...]).astype(BF16)
    acc = y
    for c in range(D_FF // FF_CHUNK):
        cols = slice(c * FF_CHUNK, (c + 1) * FF_CHUNK)
        gate = jnp.dot(h, wg_ref[:, cols], preferred_element_type=F32)
        up = jnp.dot(h, wu_ref[:, cols], preferred_element_type=F32)
        act = (jax.nn.silu(gate) * up).astype(BF16)
        acc = acc + jnp.dot(act, wd_ref[cols, :], preferred_element_type=F32)
    return acc if gf_ref is None else _rmsnorm_rows(acc, gf_ref[...])


def _mix_ffn_kernel(*refs, mix, ffn, final):
    refs = list(refs)
    out_ref = refs.pop()
    if mix:
        y = _mix_rows(*refs[:6])
        del refs[:6]
    else:
        y = refs.pop(0)[...]
    if ffn:
        y = _ffn_rows(y, *refs[:4], refs[4] if final else None)
    out_ref[...] = y


def _mix_ffn(y, *, mix=None, ffn=None, g_final=None, tm, name):
    m = y.shape[0]
    row = pl.BlockSpec((tm, D_MODEL), lambda i: (i, 0))
    args, in_specs = [y], [row]
    if mix is not None:
        o, qm, mem_kt, mem_vt, w_out, layer, rows_per_batch = mix
        assert rows_per_batch % tm == 0
        per = rows_per_batch // tm
        mem_spec = pl.BlockSpec((None, MEM_WIDTH, N_MEM), lambda i: (i // per, 0, 0))
        args += [o, qm, mem_kt, mem_vt, w_out]
        in_specs += [pl.BlockSpec((tm, SB_WIDTH), lambda i: (i, 0)),
                     pl.BlockSpec((tm, MEM_WIDTH), lambda i: (i, 0)),
                     mem_spec, mem_spec, _layer_spec(w_out, layer)]
    if ffn is not None:
        *stacked, layer = ffn
        args += stacked
        in_specs += [_layer_spec(a, layer) for a in stacked]
        if g_final is not None:
            args.append(g_final.reshape(1, D_MODEL))
            in_specs.append(_const_spec((1, D_MODEL)))
    return pl.pallas_call(
        functools.partial(_mix_ffn_kernel, mix=mix is not None, ffn=ffn is not None,
                          final=g_final is not None),
        out_shape=jax.ShapeDtypeStruct((m, D_MODEL), F32),
        grid=(m // tm,), in_specs=in_specs, out_specs=row,
        compiler_params=_cparams("arbitrary"), name=name,
    )(*args)


_PLAN_A_PROMPT = ((0, SB_WIDTH, "bf16"), (SB_WIDTH, 2 * SB_WIDTH, "f32T+bf16"),
                  (2 * SB_WIDTH, 3 * SB_WIDTH, "f32T+bf16"),
                  (3 * SB_WIDTH, 3 * SB_WIDTH + MEM_WIDTH, "bf16"))
_PLAN_A_SAMPLE = ((0, SB_WIDTH, "f32"), (SB_WIDTH, 2 * SB_WIDTH, "f32+bf16"),
                  (2 * SB_WIDTH, 3 * SB_WIDTH, "f32+bf16"),
                  (3 * SB_WIDTH, 3 * SB_WIDTH + MEM_WIDTH, "bf16"))
_PLAN_B_PROMPT = ((SGU_WIDTH, 2 * SGU_WIDTH, "gelu_norm_b"), (0, SGU_WIDTH, "gelu"),
                  (2 * SGU_WIDTH, 2 * SGU_WIDTH + MEM_WIDTH, "bf16"))
_PLAN_B_SAMPLE = ((SGU_WIDTH, 2 * SGU_WIDTH, "gelu_norm"),) + _PLAN_B_PROMPT[1:]
_PLAN_MEM = ((0, MEM_WIDTH, "f32T"), (MEM_WIDTH, 2 * MEM_WIDTH, "f32T"))


def kernel(x_prompt, x_sample, cache_sb_k, cache_sb_v, cache_mem_k, cache_mem_v,
           mem_prompt, g_mix, w_in_a, w_in_b, w_sp, b_sp, g_sgu, g_mem, w_mem_kv,
           w_out, g_ffn, w_gate, w_up, w_down, g_final):
    batch, seq, _ = x_prompt.shape
    dec_batch, dec_seq, _ = x_sample.shape
    depth = g_mix.shape[0]
    tm_p = 1024
    tm_gate = 1024
    tm_s = dec_batch * dec_seq

    y_p = x_prompt.reshape(batch * seq, D_MODEL)
    y_s = x_sample.reshape(tm_s, D_MODEL)
    mem_rows = mem_prompt.reshape(batch * N_MEM, D_MODEL)

    g_mix, g_mem, g_ffn, g_sgu = (_as_rows(g) for g in (g_mix, g_mem, g_ffn, g_sgu))
    w_in_a, w_in_b, w_mem_kv, w_out, w_gate, w_up, w_down = (
        w.astype(BF16) for w in (w_in_a, w_in_b, w_mem_kv, w_out, w_gate, w_up, w_down))

    sb_k_p, sb_v_p, sb_k_s, sb_v_s, mem_k_p, mem_v_p, sgu_v_s = ([] for _ in range(7))
    for l in range(depth):
        mkt_p, mvt_p = _norm_proj(mem_rows, (g_mem, l), (w_mem_kv, l), _PLAN_MEM,
                                  tm=N_MEM, rows_per_batch=N_MEM, name=f"mem_kv{l}")
        mem_k_p.append(_heads_minor(mkt_p, N_MEM_HEADS))
        mem_v_p.append(_heads_minor(mvt_p, N_MEM_HEADS))
        mkt_s, mvt_s = _heads_major(cache_mem_k[l]), _heads_major(cache_mem_v[l])

        if l % 2 == 0:
            ia = l // 2
            q, kt_f, k_b, vt_f, v_b, qm_p = _norm_proj(
                y_p, (g_mix, l), (w_in_a, ia), _PLAN_A_PROMPT, tm=tm_p,
                rows_per_batch=seq, name=f"in_a{l}_p")
            sb_k_p.append(_heads_minor(kt_f, N_SB_HEADS))
            sb_v_p.append(_heads_minor(vt_f, N_SB_HEADS))
            o_p = _sb_prompt(q, k_b, v_b, batch=batch, seq=seq)

            q, k_f, k_b, v_f, v_b, qm_s = _norm_proj(
                y_s, (g_mix, l), (w_in_a, ia), _PLAN_A_SAMPLE, tm=tm_s, name=f"in_a{l}_s")
            sb_k_s.append(k_f.reshape(dec_batch, dec_seq, N_SB_HEADS, HEAD_DIM))
            sb_v_s.append(v_f.reshape(dec_batch, dec_seq, N_SB_HEADS, HEAD_DIM))
            o_s = _sb_decode(q, k_b, v_b, _heads_major(cache_sb_k[ia]),
                             _heads_major(cache_sb_v[ia]),
                             batch=dec_batch, t_new=dec_seq)
        else:
            ib = l // 2
            span = min(seq, MLP_CHUNK)
            bias = jnp.repeat(b_sp[ib][:, :span].T, HEAD_DIM, axis=1)
            y_p = _gate_mix(y_p, (g_mix, l), (w_in_b, ib), (g_sgu, ib),
                            w_sp[ib][:, :span, :span], bias, mkt_p, mvt_p, (w_out, l),
                            tm=tm_gate, rows_per_batch=seq, name=f"gate_mix{l}_p")
            o_p = None

            v_f, v_b, u, qm_s = _norm_proj(
                y_s, (g_mix, l), (w_in_b, ib), _PLAN_B_SAMPLE, tm=tm_s,
                gate_norm=(g_sgu, ib), name=f"in_b{l}_s")
            sgu_v_s.append(v_f.reshape(dec_batch, dec_seq, SGU_WIDTH))
            span = min(dec_seq, MLP_CHUNK)
            reps = MLP_CHUNK // span
            eye = jnp.eye(reps, dtype=F32)
            w_bd = jnp.einsum("ab,gts->gatbs", eye, w_sp[ib][:, :span, :span]).reshape(
                -1, MLP_CHUNK, MLP_CHUNK)
            bias = jnp.tile(jnp.repeat(b_sp[ib][:, :span].T, HEAD_DIM, axis=1), (reps, 1))
            o_s = _sgu(u, v_b, w_bd, bias, tm=tm_s, name=f"sgu{l}_s")

        last = g_final if l == depth - 1 else None
        ffn = (g_ffn, w_gate, w_up, w_down, l)
        if o_p is not None:
            y_p = _mix_ffn(y_p, mix=(o_p, qm_p, mkt_p, mvt_p, w_out, l, seq),
                           tm=tm_p, name=f"mix_out{l}_p")
        y_p = _mix_ffn(y_p, ffn=ffn, g_final=last, tm=tm_p, name=f"ffn{l}_p")
        y_s = _mix_ffn(y_s, mix=(o_s, qm_s, mkt_s, mvt_s, w_out, l, dec_seq),
                       tm=dec_seq, name=f"mix_out{l}_s")
        y_s = _mix_ffn(y_s, ffn=ffn, g_final=last, tm=tm_s, name=f"ffn{l}_s")

    return (y_p.reshape(batch, seq, D_MODEL), y_s.reshape(dec_batch, dec_seq, D_MODEL),
            jnp.stack(sb_k_p), jnp.stack(sb_v_p), jnp.stack(sb_k_s), jnp.stack(sb_v_s),
            jnp.stack(mem_k_p), jnp.stack(mem_v_p), jnp.stack(sgu_v_s))
```

```python
import functools

import jax
import jax.numpy as jnp
from jax import lax
from jax.experimental import pallas as pl
from jax.experimental.pallas import tpu as pltpu

F32 = jnp.float32
BF16 = jnp.bfloat16

D_MODEL = 1024
HEAD_DIM = 64
SB_WIDTH = 768
N_SB_HEADS = SB_WIDTH // HEAD_DIM
SGU_WIDTH = 768
MLP_CHUNK = 128
MEM_WIDTH = 256
N_MEM = 256
N_MEM_HEADS = MEM_WIDTH // HEAD_DIM
D_FF = 2816
EPS = 1e-6
SB_SCALE = HEAD_DIM ** -0.5
LOG2E = 1.4426950408889634

LANES = 128
V7X_VMEM_BYTES = 64 * 1024 * 1024
VMEM_LIMIT = V7X_VMEM_BYTES - 8 * 1024 * 1024

SB_DEAD_MASS = 106.0

SB_TQ = 256
SB_TK = 256
SB_EARLY = 176
FF_CHUNK = 256
OUT_CHUNK = 256
PROJ_CHUNK = 256


def _cparams(*semantics):
    return pltpu.CompilerParams(dimension_semantics=semantics,
                                vmem_limit_bytes=VMEM_LIMIT)


def _const_spec(shape):
    zeros = (0,) * len(shape)
    return pl.BlockSpec(shape, lambda *_: zeros, pipeline_mode=pl.Buffered(1))


def _layer_spec(stacked, layer):
    return pl.BlockSpec((None,) + stacked.shape[1:], lambda *_: (layer, 0, 0),
                        pipeline_mode=pl.Buffered(1))


def _as_rows(g):
    return g.reshape(g.shape[0], 1, g.shape[1])


def _rmsnorm_rows(x, g):
    ms = jnp.mean(x * x, axis=-1, keepdims=True)
    return (x * lax.rsqrt(ms + EPS)) * g


def _dot_nt(a, b):
    return lax.dot_general(a, b, (((1,), (1,)), ((), ())), preferred_element_type=F32)


_PLAN_FORMS = {
    "bf16": ("bf16",),
    "f32": ("f32",),
    "f32+bf16": ("f32", "bf16"),
    "f32T": ("f32T",),
    "f32T+bf16": ("f32T", "bf16"),
    "gelu": ("f32",),
    "gelu_norm": ("f32", "bf16"),
    "gelu_norm_b": ("bf16",),
}


def _norm_proj_kernel(x_ref, g_ref, w_ref, *rest, plan, has_gate_norm):
    if has_gate_norm:
        gn_ref, out_refs = rest[0], rest[1:]
    else:
        gn_ref, out_refs = None, rest
    h = _rmsnorm_rows(x_ref[...], g_ref[...]).astype(BF16)
    tasks, k = [], 0
    for lo, hi, kind in plan:
        refs = out_refs[k:k + len(_PLAN_FORMS[kind])]
        k += len(refs)
        tasks += [(lo, hi, kind, refs, c, min(c + PROJ_CHUNK, hi))
                  for c in range(lo, hi, PROJ_CHUNK)]

    def product(task):
        a, b = task[4:]
        return jnp.dot(h, w_ref[:, a:b], preferred_element_type=F32)

    def store(refs, forms, a, b, value):
        for ref, form in zip(refs, forms):
            if form == "f32T":
                ref[a:b, :] = value.T
            else:
                ref[:, a:b] = value.astype(BF16 if form == "bf16" else F32)

    acts = []
    ahead = product(tasks[0])
    for n, task in enumerate(tasks):
        lo, hi, kind, refs, a, b = task
        p = ahead
        if n + 1 < len(tasks):
            ahead = product(tasks[n + 1])
        if kind.startswith("gelu"):
            p = jax.nn.gelu(p)
        if not kind.startswith("gelu_norm"):
            store(refs, _PLAN_FORMS[kind], a - lo, b - lo, p)
            continue
        acts.append((a - lo, b - lo, p))
        if b == hi:
            ms = sum(jnp.sum(t * t, axis=-1, keepdims=True) for _, _, t in acts) / (hi - lo)
            inv = lax.rsqrt(ms + EPS)
            for a0, b0, t in acts:
                store(refs, _PLAN_FORMS[kind], a0, b0, (t * inv) * gn_ref[:, a0:b0])
            acts = []


def _norm_proj(x, g, w, plan, *, tm, rows_per_batch=None, gate_norm=None, name):
    m, d = x.shape
    out_shape, out_specs = [], []
    for lo, hi, kind in plan:
        for form in _PLAN_FORMS[kind]:
            if form == "f32T":
                per = rows_per_batch // tm
                out_shape.append(jax.ShapeDtypeStruct(
                    (m // rows_per_batch, hi - lo, rows_per_batch), F32))
                out_specs.append(pl.BlockSpec(
                    (None, hi - lo, tm), lambda i, per=per: (i // per, 0, i % per)))
            else:
                out_shape.append(jax.ShapeDtypeStruct(
                    (m, hi - lo), BF16 if form == "bf16" else F32))
                out_specs.append(pl.BlockSpec((tm, hi - lo), lambda i: (i, 0)))
    layered = [g, w] + ([gate_norm] if gate_norm is not None else [])
    in_specs = [pl.BlockSpec((tm, d), lambda i: (i, 0))] + [
        _layer_spec(arr, layer) for arr, layer in layered]
    args = [x] + [arr for arr, _ in layered]
    return pl.pallas_call(
        functools.partial(_norm_proj_kernel, plan=plan,
                          has_gate_norm=gate_norm is not None),
        out_shape=out_shape, grid=(m // tm,), in_specs=in_specs,
        out_specs=out_specs, compiler_params=_cparams("arbitrary"), name=name,
    )(*args)


def _upper(n):
    j = lax.broadcasted_iota(jnp.int32, (n, n), 0)
    s = lax.broadcasted_iota(jnp.int32, (n, n), 1)
    return jnp.where(j >= s, 1.0, 0.0).astype(BF16)


def _sb_weights(z, upper, mass, causal):
    soft = jnp.maximum(z, 0.0) + jnp.log(1.0 + jnp.exp2(jnp.abs(z) * -LOG2E))
    if causal is not None:
        soft = jnp.where(causal, soft, 0.0)
    from_here = jnp.dot(soft.astype(BF16), upper, preferred_element_type=F32) + mass
    a = jnp.exp2((z - from_here) * LOG2E)
    if causal is not None:
        a = jnp.where(causal, a, 0.0)
    return a.astype(BF16), mass + jnp.sum(soft, axis=-1, keepdims=True)


def _sb_prompt_kernel(q_ref, k_ref, v_ref, y_ref, qm_ref, mk_ref, mv_ref, w_ref,
                      out_ref, acc_ref, mass_ref, *, tq, tk, early):
    i = pl.program_id(1)
    pairs = SB_WIDTH // LANES
    lane = lax.broadcasted_iota(jnp.int32, (1, LANES), 1)
    head0 = lane < HEAD_DIM
    upper2 = _upper(tk)
    mem = _mem_rows(_mem_scores(qm_ref, mk_ref), mv_ref)
    row = lax.broadcasted_iota(jnp.int32, (2 * tq, tk), 0) % tq
    col = lax.broadcasted_iota(jnp.int32, (2 * tq, tk), 1)
    has_prev = i > 0
    j_prev = jnp.maximum(i - 1, 0)

    def both_heads(x, lo, hi):
        return jnp.concatenate([x[lo:hi], x[tq + lo:tq + hi]], axis=0)

    def sweeper(p):
        lanes = slice(p * LANES, (p + 1) * LANES)
        q = q_ref[:, lanes] * jnp.asarray(SB_SCALE, BF16)
        zq = jnp.zeros_like(q)
        q2 = jnp.concatenate([jnp.where(head0, q, zq), jnp.where(head0, zq, q)], axis=0)

        def logits(qs, j):
            return _dot_nt(qs, k_ref[pl.ds(pl.multiple_of(j * tk, tk), tk), lanes])

        def values(a, j):
            n = a.shape[0] // 2
            vb = v_ref[pl.ds(pl.multiple_of(j * tk, tk), tk), lanes]
            zv = jnp.zeros_like(vb)
            return (jnp.dot(a[:n], jnp.where(head0, vb, zv), preferred_element_type=F32)
                    + jnp.dot(a[n:], jnp.where(head0, zv, vb), preferred_element_type=F32))

        return q2, logits, values

    late = tq - early
    least_early = least_late = jnp.asarray(jnp.inf, F32)
    for p in range(pairs):
        q2, logits, values = sweeper(p)
        z_diag, z_prev = logits(q2, i), logits(both_heads(q2, 0, early), j_prev)
        a_diag, mass_diag = _sb_weights(z_diag, upper2, jnp.zeros((2 * tq, 1), F32),
                                        col < row)
        mass_e = both_heads(mass_diag, 0, early)
        a_prev, mass_prev = _sb_weights(z_prev, upper2, mass_e, None)
        acc = values(a_diag, i)
        acc_early = acc[:early] + jnp.where(has_prev, values(a_prev, j_prev), 0.0)
        acc_ref[:, p * LANES:(p + 1) * LANES] = jnp.concatenate(
            [acc_early, acc[early:]], axis=0)
        mass_e = jnp.where(has_prev, mass_prev, mass_e)
        mass_l = both_heads(mass_diag, early, tq)
        mass_ref[p] = jnp.concatenate(
            [mass_e[:early], mass_l[:late], mass_e[early:], mass_l[late:]], axis=0)
        least_early = jnp.minimum(least_early, jnp.min(mass_e))
        least_late = jnp.minimum(least_late, jnp.min(mass_l))

    late_row = lax.broadcasted_iota(jnp.int32, (2 * tq, 1), 0) % tq >= early

    def cond(state):
        j, least = state
        return jnp.logical_and(j >= 0, least < SB_DEAD_MASS)

    def body(state):
        j, _ = state
        fresh = jnp.logical_or(late_row, j != i - 1)
        least = jnp.asarray(jnp.inf, F32)
        for p in range(pairs):
            q2, logits, values = sweeper(p)
            mass_old = mass_ref[p]
            a, mass = _sb_weights(logits(q2, j), upper2, mass_old, None)
            a = jnp.where(fresh, a, jnp.zeros_like(a))
            mass = jnp.where(fresh, mass, mass_old)
            acc_ref[:, p * LANES:(p + 1) * LANES] += values(a, j)
            mass_ref[p] = mass
            least = jnp.minimum(least, jnp.min(mass))
        return j - 1, least

    j_start = jnp.where(least_late < SB_DEAD_MASS, i - 1, i - 2)
    lax.while_loop(cond, body, (j_start, jnp.minimum(least_early, least_late)))
    mixer_part = _mixer_part(acc_ref[...].astype(BF16), w_ref)
    out_ref[...] = _out_rows(y_ref, mixer_part, mem, w_ref)


def _sb_prompt_mix(y, q, k, v, qm, mem_kt, mem_vt, w_out, *, batch, seq, name):
    tq, tk = SB_TQ, SB_TK
    assert tq == tk and seq % tq == 0
    per = seq // tq
    q3, k3, v3 = (a.reshape(batch, seq, SB_WIDTH) for a in (q, k, v))
    q_spec = pl.BlockSpec((None, tq, SB_WIDTH), lambda b, i: (b, i, 0))
    kv_spec = pl.BlockSpec((None, seq, SB_WIDTH), lambda b, i: (b, 0, 0))
    row = lambda width: pl.BlockSpec((tq, width), lambda b, i: (b * per + i, 0))
    mem_spec = pl.BlockSpec((None, MEM_WIDTH, N_MEM), lambda b, i: (b, 0, 0))
    return pl.pallas_call(
        functools.partial(_sb_prompt_kernel, tq=tq, tk=tk, early=SB_EARLY),
        out_shape=jax.ShapeDtypeStruct((batch * seq, D_MODEL), F32),
        grid=(batch, per),
        in_specs=[q_spec, kv_spec, kv_spec, row(D_MODEL), row(MEM_WIDTH),
                  mem_spec, mem_spec, _layer_spec(*w_out)],
        out_specs=row(D_MODEL),
        scratch_shapes=[pltpu.VMEM((tq, SB_WIDTH), F32),
                        pltpu.VMEM((SB_WIDTH // LANES, 2 * tq, 1), F32)],
        compiler_params=_cparams("arbitrary", "arbitrary"),
        name=name,
    )(q3, k3, v3, y, qm, mem_kt, mem_vt, w_out[0])


def _sb_decode_kernel(q_ref, kn_ref, vn_ref, kc_hbm, vc_hbm, o_ref,
                      k_buf, v_buf, sems, acc_ref, *, t_new, tk, n_blocks):
    b = pl.program_id(0)
    n_rows = N_SB_HEADS * t_new

    def block_copies(j, slot):
        keys = pl.ds(pl.multiple_of(j * tk, tk), tk)
        return (pltpu.make_async_copy(kc_hbm.at[b, :, keys], k_buf.at[slot], sems.at[0, slot]),
                pltpu.make_async_copy(vc_hbm.at[b, :, keys], v_buf.at[slot], sems.at[1, slot]))

    def fetch(j, slot):
        for copy in block_copies(j, slot):
            copy.start()

    def arrive(j, slot):
        for copy in block_copies(j, slot):
            copy.wait()

    last = n_blocks - 1
    fetch(last, last % 2)

    row_head = lax.broadcasted_iota(jnp.int32, (n_rows, SB_WIDTH), 0) // t_new
    col_head = lax.broadcasted_iota(jnp.int32, (n_rows, SB_WIDTH), 1) // HEAD_DIM
    q_rows = jnp.concatenate([q_ref[...]] * N_SB_HEADS, axis=0)
    q_bd = jnp.where(row_head == col_head, q_rows * SB_SCALE, 0.0).astype(BF16)

    n_pad = kn_ref.shape[0]
    q_idx = lax.broadcasted_iota(jnp.int32, (n_rows, n_pad), 0) % t_new
    s_idx = lax.broadcasted_iota(jnp.int32, (n_rows, n_pad), 1)
    a, mass = _sb_weights(_dot_nt(q_bd, kn_ref[...]), _upper(n_pad),
                          jnp.zeros((n_rows, 1), F32), s_idx < q_idx)
    acc_ref[...] = jnp.dot(a, vn_ref[...], preferred_element_type=F32)
    upper2 = _upper(tk)

    def cond(state):
        j, _, least = state
        return jnp.logical_and(j >= 0, least < SB_DEAD_MASS)

    def body(state):
        j, mass, _ = state
        slot = lax.rem(j, 2)
        arrive(j, slot)

        @pl.when(j > 0)
        def _():
            fetch(j - 1, 1 - slot)

        z = jnp.dot(q_bd, k_buf[slot].astype(BF16), preferred_element_type=F32)
        a, mass = _sb_weights(z, upper2, mass, None)
        acc_ref[...] += _dot_nt(a, v_buf[slot].astype(BF16))
        return j - 1, mass, jnp.min(mass)

    j_end, _, _ = lax.while_loop(cond, body, (last, mass, jnp.min(mass)))

    @pl.when(j_end >= 0)
    def _():
        arrive(j_end, lax.rem(j_end, 2))

    out_head = lax.broadcasted_iota(jnp.int32, (t_new, SB_WIDTH), 1) // HEAD_DIM
    out = jnp.zeros((t_new, SB_WIDTH), F32)
    for h in range(N_SB_HEADS):
        out = out + jnp.where(out_head == h, acc_ref[h * t_new:(h + 1) * t_new, :], 0.0)
    o_ref[...] = out.astype(o_ref.dtype)


def _sb_decode(q, k_new, v_new, cache_kt, cache_vt, *, batch, t_new):
    past = cache_kt.shape[2]
    tk = SB_TK
    assert past % tk == 0
    pad = ((0, 0), (0, LANES - t_new), (0, 0))
    kn = jnp.pad(k_new.reshape(batch, t_new, SB_WIDTH), pad)
    vn = jnp.pad(v_new.reshape(batch, t_new, SB_WIDTH), pad)
    row_spec = pl.BlockSpec((None, t_new, SB_WIDTH), lambda b: (b, 0, 0))
    new_spec = pl.BlockSpec((None, LANES, SB_WIDTH), lambda b: (b, 0, 0))
    hbm_spec = pl.BlockSpec(memory_space=pl.ANY)
    out = pl.pallas_call(
        functools.partial(_sb_decode_kernel, t_new=t_new, tk=tk, n_blocks=past // tk),
        out_shape=jax.ShapeDtypeStruct((batch, t_new, SB_WIDTH), BF16),
        grid=(batch,),
        in_specs=[row_spec, new_spec, new_spec, hbm_spec, hbm_spec],
        out_specs=row_spec,
        scratch_shapes=[pltpu.VMEM((2, SB_WIDTH, tk), F32),
                        pltpu.VMEM((2, SB_WIDTH, tk), F32),
                        pltpu.SemaphoreType.DMA((2, 2)),
                        pltpu.VMEM((N_SB_HEADS * t_new, SB_WIDTH), F32)],
        compiler_params=_cparams("arbitrary"),
        name="sb_decode",
    )(q.reshape(batch, t_new, SB_WIDTH), kn, vn, cache_kt, cache_vt)
    return out.reshape(batch * t_new, SB_WIDTH)


def _heads_major(x):
    b, t, h, d = x.shape
    return jnp.transpose(x, (0, 2, 3, 1)).reshape(b, h * d, t)


def _heads_minor(xt, heads):
    b, hd, t = xt.shape
    return jnp.transpose(xt.reshape(b, heads, hd // heads, t), (0, 3, 1, 2))


def _sgu_kernel(u_ref, v_ref, w_ref, b_ref, o_ref, *, span, n_chunks):
    lane = lax.broadcasted_iota(jnp.int32, (1, LANES), 1)
    group0 = lane < HEAD_DIM
    t = lax.broadcasted_iota(jnp.int32, (span, span), 0)
    s = lax.broadcasted_iota(jnp.int32, (span, span), 1)
    tril = s <= t
    for p in range(SGU_WIDTH // LANES):
        w_pair = jnp.concatenate(
            [jnp.where(tril, w_ref[2 * p], 0.0), jnp.where(tril, w_ref[2 * p + 1], 0.0)],
            axis=1).astype(BF16)
        cols = slice(p * LANES, (p + 1) * LANES)
        for c in range(n_chunks):
            rows = slice(c * span, (c + 1) * span)
            vv = v_ref[rows, cols]
            zv = jnp.zeros_like(vv)
            v_pair = jnp.concatenate([jnp.where(group0, vv, zv),
                                      jnp.where(group0, zv, vv)], axis=0)
            mixed = jnp.dot(w_pair, v_pair, preferred_element_type=F32) + b_ref[:, cols]
            o_ref[rows, cols] = (u_ref[rows, cols] * mixed).astype(o_ref.dtype)


def _sgu(u, v, w, bias, *, tm, name):
    m = u.shape[0]
    span = w.shape[-1]
    blk = pl.BlockSpec((tm, SGU_WIDTH), lambda i: (i, 0))
    return pl.pallas_call(
        functools.partial(_sgu_kernel, span=span, n_chunks=tm // span),
        out_shape=jax.ShapeDtypeStruct((m, SGU_WIDTH), BF16),
        grid=(m // tm,),
        in_specs=[blk, blk, _const_spec(w.shape), _const_spec(bias.shape)],
        out_specs=blk, compiler_params=_cparams("arbitrary"), name=name,
    )(u, v, w, bias)


_OUT_CHUNKS = [slice(c, c + OUT_CHUNK) for c in range(0, D_MODEL, OUT_CHUNK)]


def _mem_scores(qm_ref, mk_ref):
    lane = lax.broadcasted_iota(jnp.int32, (1, MEM_WIDTH), 1) // HEAD_DIM
    qm = qm_ref[...] * jnp.asarray(SB_SCALE, BF16)
    zq = jnp.zeros_like(qm)
    mk = mk_ref[...].astype(BF16)
    return [jnp.dot(jnp.where(lane == h, qm, zq), mk, preferred_element_type=F32)
            for h in range(N_MEM_HEADS)]


def _mem_rows(scores, mv_ref):
    sublane = lax.broadcasted_iota(jnp.int32, (MEM_WIDTH, 1), 0) // HEAD_DIM
    mv = mv_ref[...].astype(BF16)
    zv = jnp.zeros_like(mv)
    mem = 0.0
    for h, s in enumerate(scores):
        e = jnp.exp(s - jnp.max(s, axis=-1, keepdims=True))
        p = e / jnp.sum(e, axis=-1, keepdims=True)
        mem = mem + _dot_nt(p.astype(BF16), jnp.where(sublane == h, mv, zv))
    return mem.astype(BF16)


def _mixer_part(o, w_ref):
    return [jnp.dot(o, w_ref[:o.shape[1], cols], preferred_element_type=F32)
            for cols in _OUT_CHUNKS]


def _out_rows(y_ref, mixer_part, mem, w_ref):
    width = D_MODEL - mem.shape[1]
    return jnp.concatenate(
        [(y_ref[:, cols] + part) + jnp.dot(mem, w_ref[width:, cols],
                                           preferred_element_type=F32)
         for cols, part in zip(_OUT_CHUNKS, mixer_part)], axis=1)


def _mix_rows(y_ref, o_ref, qm_ref, mk_ref, mv_ref, w_ref):
    scores = _mem_scores(qm_ref, mk_ref)
    mixer_part = _mixer_part(o_ref[...], w_ref)
    return _out_rows(y_ref, mixer_part, _mem_rows(scores, mv_ref), w_ref)


def _gate_mix_kernel(y_ref, g_ref, w_ref, gn_ref, wsp_ref, b_ref, mk_ref, mv_ref,
                     wout_ref, out_ref, o_scr, *, span):
    h = _rmsnorm_rows(y_ref[...], g_ref[...]).astype(BF16)

    def proj(lo, hi):
        return [jnp.dot(h, w_ref[:, a:min(a + PROJ_CHUNK, hi)], preferred_element_type=F32)
                for a in range(lo, hi, PROJ_CHUNK)]

    acts = [jax.nn.gelu(p) for p in proj(SGU_WIDTH, 2 * SGU_WIDTH)]
    ms = sum(jnp.sum(t * t, axis=-1, keepdims=True) for t in acts) / SGU_WIDTH
    v = (jnp.concatenate(acts, axis=1) * lax.rsqrt(ms + EPS)) * gn_ref[...]
    u = jnp.concatenate([jax.nn.gelu(p) for p in proj(0, SGU_WIDTH)], axis=1)
    (qm,) = proj(2 * SGU_WIDTH, 2 * SGU_WIDTH + MEM_WIDTH)
    _sgu_kernel(u, v.astype(BF16), wsp_ref, b_ref, o_scr, span=span,
                n_chunks=y_ref.shape[0] // span)
    out_ref[...] = _mix_rows(y_ref, o_scr, qm.astype(BF16), mk_ref, mv_ref, wout_ref)


def _gate_mix(y, g, w_in, gate_norm, w_sp, bias, mem_kt, mem_vt, w_out, *,
              tm, rows_per_batch, name):
    m = y.shape[0]
    span = w_sp.shape[-1]
    assert rows_per_batch % tm == 0 and tm % span == 0
    per = rows_per_batch // tm
    row = pl.BlockSpec((tm, D_MODEL), lambda i: (i, 0))
    mem_spec = pl.BlockSpec((None, MEM_WIDTH, N_MEM), lambda i: (i // per, 0, 0))
    layered = [g, w_in, gate_norm]
    return pl.pallas_call(
        functools.partial(_gate_mix_kernel, span=span),
        out_shape=jax.ShapeDtypeStruct((m, D_MODEL), F32),
        grid=(m // tm,),
        in_specs=[row] + [_layer_spec(*pair) for pair in layered] + [
            _const_spec(w_sp.shape), _const_spec(bias.shape), mem_spec, mem_spec,
            _layer_spec(*w_out)],
        out_specs=row,
        scratch_shapes=[pltpu.VMEM((tm, SGU_WIDTH), BF16)],
        compiler_params=_cparams("arbitrary"), name=name,
    )(y, *[arr for arr, _ in layered], w_sp, bias, mem_kt, mem_vt, w_out[0])


def _ffn_rows(y, g_ref, wg_ref, wu_ref, wd_ref, gf_ref):
    h = _rmsnorm_rows(y, g_ref[...]).astype(BF16)
    acc = y
    for c in range(D_FF // FF_CHUNK):
        cols = slice(c * FF_CHUNK, (c + 1) * FF_CHUNK)
        gate = jnp.dot(h, wg_ref[:, cols], preferred_element_type=F32)
        up = jnp.dot(h, wu_ref[:, cols], preferred_element_type=F32)
        act = (jax.nn.silu(gate) * up).astype(BF16)
        acc = acc + jnp.dot(act, wd_ref[cols, :], preferred_element_type=F32)
    return acc if gf_ref is None else _rmsnorm_rows(acc, gf_ref[...])


def _mix_ffn_kernel(*refs, mix, ffn, final):
    refs = list(refs)
    out_ref = refs.pop()
    if mix:
        y = _mix_rows(*refs[:6])
        del refs[:6]
    else:
        y = refs.pop(0)[...]
    if ffn:
        y = _ffn_rows(y, *refs[:4], refs[4] if final else None)
    out_ref[...] = y


def _mix_ffn(y, *, mix=None, ffn=None, g_final=None, tm, name):
    m = y.shape[0]
    row = pl.BlockSpec((tm, D_MODEL), lambda i: (i, 0))
    args, in_specs = [y], [row]
    if mix is not None:
        o, qm, mem_kt, mem_vt, w_out, layer, rows_per_batch = mix
        assert rows_per_batch % tm == 0
        per = rows_per_batch // tm
        mem_spec = pl.BlockSpec((None, MEM_WIDTH, N_MEM), lambda i: (i // per, 0, 0))
        args += [o, qm, mem_kt, mem_vt, w_out]
        in_specs += [pl.BlockSpec((tm, SB_WIDTH), lambda i: (i, 0)),
                     pl.BlockSpec((tm, MEM_WIDTH), lambda i: (i, 0)),
                     mem_spec, mem_spec, _layer_spec(w_out, layer)]
    if ffn is not None:
        *stacked, layer = ffn
        args += stacked
        in_specs += [_layer_spec(a, layer) for a in stacked]
        if g_final is not None:
            args.append(g_final.reshape(1, D_MODEL))
            in_specs.append(_const_spec((1, D_MODEL)))
    return pl.pallas_call(
        functools.partial(_mix_ffn_kernel, mix=mix is not None, ffn=ffn is not None,
                          final=g_final is not None),
        out_shape=jax.ShapeDtypeStruct((m, D_MODEL), F32),
        grid=(m // tm,), in_specs=in_specs, out_specs=row,
        compiler_params=_cparams("arbitrary"), name=name,
    )(*args)


_PLAN_A_PROMPT = ((0, SB_WIDTH, "bf16"), (SB_WIDTH, 2 * SB_WIDTH, "f32T+bf16"),
                  (2 * SB_WIDTH, 3 * SB_WIDTH, "f32T+bf16"),
                  (3 * SB_WIDTH, 3 * SB_WIDTH + MEM_WIDTH, "bf16"))
_PLAN_A_SAMPLE = ((0, SB_WIDTH, "f32"), (SB_WIDTH, 2 * SB_WIDTH, "f32+bf16"),
                  (2 * SB_WIDTH, 3 * SB_WIDTH, "f32+bf16"),
                  (3 * SB_WIDTH, 3 * SB_WIDTH + MEM_WIDTH, "bf16"))
_PLAN_B_PROMPT = ((SGU_WIDTH, 2 * SGU_WIDTH, "gelu_norm_b"), (0, SGU_WIDTH, "gelu"),
                  (2 * SGU_WIDTH, 2 * SGU_WIDTH + MEM_WIDTH, "bf16"))
_PLAN_B_SAMPLE = ((SGU_WIDTH, 2 * SGU_WIDTH, "gelu_norm"),) + _PLAN_B_PROMPT[1:]
_PLAN_MEM = ((0, MEM_WIDTH, "f32T"), (MEM_WIDTH, 2 * MEM_WIDTH, "f32T"))


def kernel(x_prompt, x_sample, cache_sb_k, cache_sb_v, cache_mem_k, cache_mem_v,
           mem_prompt, g_mix, w_in_a, w_in_b, w_sp, b_sp, g_sgu, g_mem, w_mem_kv,
           w_out, g_ffn, w_gate, w_up, w_down, g_final):
    batch, seq, _ = x_prompt.shape
    dec_batch, dec_seq, _ = x_sample.shape
    depth = g_mix.shape[0]
    tm_p = 1024
    tm_gate = 1024
    tm_s = dec_batch * dec_seq

    y_p = x_prompt.reshape(batch * seq, D_MODEL)
    y_s = x_sample.reshape(tm_s, D_MODEL)
    mem_rows = mem_prompt.reshape(batch * N_MEM, D_MODEL)

    g_mix, g_mem, g_ffn, g_sgu = (_as_rows(g) for g in (g_mix, g_mem, g_ffn, g_sgu))
    w_in_a, w_in_b, w_mem_kv, w_out, w_gate, w_up, w_down = (
        w.astype(BF16) for w in (w_in_a, w_in_b, w_mem_kv, w_out, w_gate, w_up, w_down))

    sb_k_p, sb_v_p, sb_k_s, sb_v_s, mem_k_p, mem_v_p, sgu_v_s = ([] for _ in range(7))
    for l in range(depth):
        mkt_p, mvt_p = _norm_proj(mem_rows, (g_mem, l), (w_mem_kv, l), _PLAN_MEM,
                                  tm=N_MEM, rows_per_batch=N_MEM, name=f"mem_kv{l}")
        mem_k_p.append(_heads_minor(mkt_p, N_MEM_HEADS))
        mem_v_p.append(_heads_minor(mvt_p, N_MEM_HEADS))
        mkt_s, mvt_s = _heads_major(cache_mem_k[l]), _heads_major(cache_mem_v[l])

        if l % 2 == 0:
            ia = l // 2
            q, kt_f, k_b, vt_f, v_b, qm_p = _norm_proj(
                y_p, (g_mix, l), (w_in_a, ia), _PLAN_A_PROMPT, tm=tm_p,
                rows_per_batch=seq, name=f"in_a{l}_p")
            sb_k_p.append(_heads_minor(kt_f, N_SB_HEADS))
            sb_v_p.append(_heads_minor(vt_f, N_SB_HEADS))
            y_p = _sb_prompt_mix(y_p, q, k_b, v_b, qm_p, mkt_p, mvt_p, (w_out, l),
                                 batch=batch, seq=seq, name=f"sb_mix{l}_p")

            q, k_f, k_b, v_f, v_b, qm_s = _norm_proj(
                y_s, (g_mix, l), (w_in_a, ia), _PLAN_A_SAMPLE, tm=tm_s, name=f"in_a{l}_s")
            sb_k_s.append(k_f.reshape(dec_batch, dec_seq, N_SB_HEADS, HEAD_DIM))
            sb_v_s.append(v_f.reshape(dec_batch, dec_seq, N_SB_HEADS, HEAD_DIM))
            o_s = _sb_decode(q, k_b, v_b, _heads_major(cache_sb_k[ia]),
                             _heads_major(cache_sb_v[ia]),
                             batch=dec_batch, t_new=dec_seq)
        else:
            ib = l // 2
            span = min(seq, MLP_CHUNK)
            bias = jnp.repeat(b_sp[ib][:, :span].T, HEAD_DIM, axis=1)
            y_p = _gate_mix(y_p, (g_mix, l), (w_in_b, ib), (g_sgu, ib),
                            w_sp[ib][:, :span, :span], bias, mkt_p, mvt_p, (w_out, l),
                            tm=tm_gate, rows_per_batch=seq, name=f"gate_mix{l}_p")

            v_f, v_b, u, qm_s = _norm_proj(
                y_s, (g_mix, l), (w_in_b, ib), _PLAN_B_SAMPLE, tm=tm_s,
                gate_norm=(g_sgu, ib), name=f"in_b{l}_s")
            sgu_v_s.append(v_f.reshape(dec_batch, dec_seq, SGU_WIDTH))
            span = min(dec_seq, MLP_CHUNK)
            reps = MLP_CHUNK // span
            eye = jnp.eye(reps, dtype=F32)
            w_bd = jnp.einsum("ab,gts->gatbs", eye, w_sp[ib][:, :span, :span]).reshape(
                -1, MLP_CHUNK, MLP_CHUNK)
            bias = jnp.tile(jnp.repeat(b_sp[ib][:, :span].T, HEAD_DIM, axis=1), (reps, 1))
            o_s = _sgu(u, v_b, w_bd, bias, tm=tm_s, name=f"sgu{l}_s")

        last = g_final if l == depth - 1 else None
        ffn = (g_ffn, w_gate, w_up, w_down, l)
        y_p = _mix_ffn(y_p, ffn=ffn, g_final=last, tm=tm_p, name=f"ffn{l}_p")
        y_s = _mix_ffn(y_s, mix=(o_s, qm_s, mkt_s, mvt_s, w_out, l, dec_seq),
                       tm=dec_seq, name=f"mix_out{l}_s")
        y_s = _mix_ffn(y_s, ffn=ffn, g_final=last, tm=tm_s, name=f"ffn{l}_s")

    return (y_p.reshape(batch, seq, D_MODEL), y_s.reshape(dec_batch, dec_seq, D_MODEL),
            jnp.stack(sb_k_p), jnp.stack(sb_v_p), jnp.stack(sb_k_s), jnp.stack(sb_v_s),
            jnp.stack(mem_k_p), jnp.stack(mem_v_p), jnp.stack(sgu_v_s))
```

```python
import functools

import jax
import jax.numpy as jnp
from jax import lax
from jax.experimental import pallas as pl
from jax.experimental.pallas import tpu as pltpu

F32 = jnp.float32
BF16 = jnp.bfloat16

D_MODEL = 1024
HEAD_DIM = 64
SB_WIDTH = 768
N_SB_HEADS = SB_WIDTH // HEAD_DIM
SGU_WIDTH = 768
MLP_CHUNK = 128
MEM_WIDTH = 256
N_MEM = 256
N_MEM_HEADS = MEM_WIDTH // HEAD_DIM
D_FF = 2816
EPS = 1e-6
SB_SCALE = HEAD_DIM ** -0.5
LOG2E = 1.4426950408889634

LANES = 128
V7X_VMEM_BYTES = 64 * 1024 * 1024
VMEM_LIMIT = V7X_VMEM_BYTES - 8 * 1024 * 1024

SB_DEAD_MASS = 106.0

SB_TQ = 256
SB_TK = 256
SB_EARLY = 160
FF_CHUNK = 256
OUT_CHUNK = 256
PROJ_CHUNK = 256


def _cparams(*semantics):
    return pltpu.CompilerParams(dimension_semantics=semantics,
                                vmem_limit_bytes=VMEM_LIMIT)


def _const_spec(shape):
    zeros = (0,) * len(shape)
    return pl.BlockSpec(shape, lambda *_: zeros, pipeline_mode=pl.Buffered(1))


def _layer_spec(stacked, layer):
    return pl.BlockSpec((None,) + stacked.shape[1:], lambda *_: (layer, 0, 0),
                        pipeline_mode=pl.Buffered(1))


def _as_rows(g):
    return g.reshape(g.shape[0], 1, g.shape[1])


def _rmsnorm_rows(x, g):
    ms = jnp.mean(x * x, axis=-1, keepdims=True)
    return (x * lax.rsqrt(ms + EPS)) * g


def _dot_nt(a, b):
    return lax.dot_general(a, b, (((1,), (1,)), ((), ())), preferred_element_type=F32)


_PLAN_FORMS = {
    "bf16": ("bf16",),
    "f32": ("f32",),
    "f32+bf16": ("f32", "bf16"),
    "f32T": ("f32T",),
    "f32T+bf16": ("f32T", "bf16"),
    "gelu": ("f32",),
    "gelu_norm": ("f32", "bf16"),
    "gelu_norm_b": ("bf16",),
}


def _norm_proj_kernel(x_ref, g_ref, w_ref, *rest, plan, has_gate_norm):
    if has_gate_norm:
        gn_ref, out_refs = rest[0], rest[1:]
    else:
        gn_ref, out_refs = None, rest
    h = _rmsnorm_rows(x_ref[...], g_ref[...]).astype(BF16)
    tasks, k = [], 0
    for lo, hi, kind in plan:
        refs = out_refs[k:k + len(_PLAN_FORMS[kind])]
        k += len(refs)
        tasks += [(lo, hi, kind, refs, c, min(c + PROJ_CHUNK, hi))
                  for c in range(lo, hi, PROJ_CHUNK)]

    def product(task):
        a, b = task[4:]
        return jnp.dot(h, w_ref[:, a:b], preferred_element_type=F32)

    def store(refs, forms, a, b, value):
        for ref, form in zip(refs, forms):
            if form == "f32T":
                ref[a:b, :] = value.T
            else:
                ref[:, a:b] = value.astype(BF16 if form == "bf16" else F32)

    acts = []
    ahead = product(tasks[0])
    for n, task in enumerate(tasks):
        lo, hi, kind, refs, a, b = task
        p = ahead
        if n + 1 < len(tasks):
            ahead = product(tasks[n + 1])
        if kind.startswith("gelu"):
            p = jax.nn.gelu(p)
        if not kind.startswith("gelu_norm"):
            store(refs, _PLAN_FORMS[kind], a - lo, b - lo, p)
            continue
        acts.append((a - lo, b - lo, p))
        if b == hi:
            ms = sum(jnp.sum(t * t, axis=-1, keepdims=True) for _, _, t in acts) / (hi - lo)
            inv = lax.rsqrt(ms + EPS)
            for a0, b0, t in acts:
                store(refs, _PLAN_FORMS[kind], a0, b0, (t * inv) * gn_ref[:, a0:b0])
            acts = []


def _norm_proj(x, g, w, plan, *, tm, rows_per_batch=None, gate_norm=None, name):
    m, d = x.shape
    out_shape, out_specs = [], []
    for lo, hi, kind in plan:
        for form in _PLAN_FORMS[kind]:
            if form == "f32T":
                per = rows_per_batch // tm
                out_shape.append(jax.ShapeDtypeStruct(
                    (m // rows_per_batch, hi - lo, rows_per_batch), F32))
                out_specs.append(pl.BlockSpec(
                    (None, hi - lo, tm), lambda i, per=per: (i // per, 0, i % per)))
            else:
                out_shape.append(jax.ShapeDtypeStruct(
                    (m, hi - lo), BF16 if form == "bf16" else F32))
                out_specs.append(pl.BlockSpec((tm, hi - lo), lambda i: (i, 0)))
    layered = [g, w] + ([gate_norm] if gate_norm is not None else [])
    in_specs = [pl.BlockSpec((tm, d), lambda i: (i, 0))] + [
        _layer_spec(arr, layer) for arr, layer in layered]
    args = [x] + [arr for arr, _ in layered]
    return pl.pallas_call(
        functools.partial(_norm_proj_kernel, plan=plan,
                          has_gate_norm=gate_norm is not None),
        out_shape=out_shape, grid=(m // tm,), in_specs=in_specs,
        out_specs=out_specs, compiler_params=_cparams("arbitrary"), name=name,
    )(*args)


def _upper(n):
    j = lax.broadcasted_iota(jnp.int32, (n, n), 0)
    s = lax.broadcasted_iota(jnp.int32, (n, n), 1)
    return jnp.where(j >= s, 1.0, 0.0).astype(BF16)


def _sb_weights(z, upper, mass, causal):
    soft = jnp.maximum(z, 0.0) + jnp.log(1.0 + jnp.exp2(jnp.abs(z) * -LOG2E))
    if causal is not None:
        soft = jnp.where(causal, soft, 0.0)
    from_here = jnp.dot(soft.astype(BF16), upper, preferred_element_type=F32) + mass
    a = jnp.exp2((z - from_here) * LOG2E)
    if causal is not None:
        a = jnp.where(causal, a, 0.0)
    return a.astype(BF16), mass + jnp.sum(soft, axis=-1, keepdims=True)


def _sb_prompt_kernel(q_ref, k_ref, v_ref, y_ref, qm_ref, mk_ref, mv_ref, w_ref,
                      out_ref, acc_ref, mass_ref, *, tq, tk, early):
    i = pl.program_id(1)
    pairs = SB_WIDTH // LANES
    lane = lax.broadcasted_iota(jnp.int32, (1, LANES), 1)
    head0 = lane < HEAD_DIM
    upper2 = _upper(tk)
    mem = _mem_rows(_mem_scores(qm_ref, mk_ref), mv_ref)
    row = lax.broadcasted_iota(jnp.int32, (2 * tq, tk), 0) % tq
    col = lax.broadcasted_iota(jnp.int32, (2 * tq, tk), 1)
    has_prev = i > 0
    j_prev = jnp.maximum(i - 1, 0)

    def both_heads(x, lo, hi):
        return jnp.concatenate([x[lo:hi], x[tq + lo:tq + hi]], axis=0)

    def sweeper(p):
        lanes = slice(p * LANES, (p + 1) * LANES)
        q = q_ref[:, lanes] * jnp.asarray(SB_SCALE, BF16)
        zq = jnp.zeros_like(q)
        q2 = jnp.concatenate([jnp.where(head0, q, zq), jnp.where(head0, zq, q)], axis=0)

        def logits(qs, j):
            return _dot_nt(qs, k_ref[pl.ds(pl.multiple_of(j * tk, tk), tk), lanes])

        def values(a, j):
            n = a.shape[0] // 2
            vb = v_ref[pl.ds(pl.multiple_of(j * tk, tk), tk), lanes]
            zv = jnp.zeros_like(vb)
            return (jnp.dot(a[:n], jnp.where(head0, vb, zv), preferred_element_type=F32)
                    + jnp.dot(a[n:], jnp.where(head0, zv, vb), preferred_element_type=F32))

        return q2, logits, values

    late = tq - early
    least_early = least_late = jnp.asarray(jnp.inf, F32)
    for p in range(pairs):
        q2, logits, values = sweeper(p)
        z_diag, z_prev = logits(q2, i), logits(both_heads(q2, 0, early), j_prev)
        a_diag, mass_diag = _sb_weights(z_diag, upper2, jnp.zeros((2 * tq, 1), F32),
                                        col < row)
        mass_e = both_heads(mass_diag, 0, early)
        a_prev, mass_prev = _sb_weights(z_prev, upper2, mass_e, None)
        acc = values(a_diag, i)
        acc_early = acc[:early] + jnp.where(has_prev, values(a_prev, j_prev), 0.0)
        acc_ref[:, p * LANES:(p + 1) * LANES] = jnp.concatenate(
            [acc_early, acc[early:]], axis=0)
        mass_e = jnp.where(has_prev, mass_prev, mass_e)
        mass_l = both_heads(mass_diag, early, tq)
        mass_ref[p] = jnp.concatenate(
            [mass_e[:early], mass_l[:late], mass_e[early:], mass_l[late:]], axis=0)
        least_early = jnp.minimum(least_early, jnp.min(mass_e))
        least_late = jnp.minimum(least_late, jnp.min(mass_l))

    late_row = lax.broadcasted_iota(jnp.int32, (2 * tq, 1), 0) % tq >= early

    def cond(state):
        j, least = state
        return jnp.logical_and(j >= 0, least < SB_DEAD_MASS)

    def body(state):
        j, _ = state
        fresh = jnp.logical_or(late_row, j != i - 1)
        least = jnp.asarray(jnp.inf, F32)
        for p in range(pairs):
            q2, logits, values = sweeper(p)
            mass_old = mass_ref[p]
            a, mass = _sb_weights(logits(q2, j), upper2, mass_old, None)
            a = jnp.where(fresh, a, jnp.zeros_like(a))
            mass = jnp.where(fresh, mass, mass_old)
            acc_ref[:, p * LANES:(p + 1) * LANES] += values(a, j)
            mass_ref[p] = mass
            least = jnp.minimum(least, jnp.min(mass))
        return j - 1, least

    j_start = jnp.where(least_late < SB_DEAD_MASS, i - 1, i - 2)
    lax.while_loop(cond, body, (j_start, jnp.minimum(least_early, least_late)))
    mixer_part = _mixer_part(acc_ref[...].astype(BF16), w_ref)
    out_ref[...] = _out_rows(y_ref, mixer_part, mem, w_ref)


def _sb_prompt_mix(y, q, k, v, qm, mem_kt, mem_vt, w_out, *, batch, seq, name):
    tq, tk = SB_TQ, SB_TK
    assert tq == tk and seq % tq == 0
    per = seq // tq
    q3, k3, v3 = (a.reshape(batch, seq, SB_WIDTH) for a in (q, k, v))
    q_spec = pl.BlockSpec((None, tq, SB_WIDTH), lambda b, i: (b, i, 0))
    kv_spec = pl.BlockSpec((None, seq, SB_WIDTH), lambda b, i: (b, 0, 0))
    row = lambda width: pl.BlockSpec((tq, width), lambda b, i: (b * per + i, 0))
    mem_spec = pl.BlockSpec((None, MEM_WIDTH, N_MEM), lambda b, i: (b, 0, 0))
    return pl.pallas_call(
        functools.partial(_sb_prompt_kernel, tq=tq, tk=tk, early=SB_EARLY),
        out_shape=jax.ShapeDtypeStruct((batch * seq, D_MODEL), F32),
        grid=(batch, per),
        in_specs=[q_spec, kv_spec, kv_spec, row(D_MODEL), row(MEM_WIDTH),
                  mem_spec, mem_spec, _layer_spec(*w_out)],
        out_specs=row(D_MODEL),
        scratch_shapes=[pltpu.VMEM((tq, SB_WIDTH), F32),
                        pltpu.VMEM((SB_WIDTH // LANES, 2 * tq, 1), F32)],
        compiler_params=_cparams("arbitrary", "arbitrary"),
        name=name,
    )(q3, k3, v3, y, qm, mem_kt, mem_vt, w_out[0])


def _sb_decode_kernel(q_ref, kn_ref, vn_ref, kc_hbm, vc_hbm, o_ref,
                      k_buf, v_buf, sems, acc_ref, *, t_new, tk, n_blocks):
    b = pl.program_id(0)
    n_rows = N_SB_HEADS * t_new
    last = n_blocks - 1

    def block_copies(stream, j, slot):
        keys = pl.ds(pl.multiple_of(j * tk, tk), tk)
        return (pltpu.make_async_copy(kc_hbm.at[stream, :, keys], k_buf.at[slot],
                                      sems.at[0, slot]),
                pltpu.make_async_copy(vc_hbm.at[stream, :, keys], v_buf.at[slot],
                                      sems.at[1, slot]))

    def fetch(stream, j, slot):
        for copy in block_copies(stream, j, slot):
            copy.start()

    def arrive(stream, j, slot):
        for copy in block_copies(stream, j, slot):
            copy.wait()

    @pl.when(b == 0)
    def _():
        fetch(b, last, 0)

    @pl.when(b + 1 < pl.num_programs(0))
    def _():
        fetch(b + 1, last, lax.rem(b + 1, 2))

    row_head = lax.broadcasted_iota(jnp.int32, (n_rows, SB_WIDTH), 0) // t_new
    col_head = lax.broadcasted_iota(jnp.int32, (n_rows, SB_WIDTH), 1) // HEAD_DIM
    q_rows = jnp.concatenate([q_ref[...]] * N_SB_HEADS, axis=0)
    q_bd = jnp.where(row_head == col_head, q_rows * SB_SCALE, 0.0).astype(BF16)

    n_pad = kn_ref.shape[0]
    q_idx = lax.broadcasted_iota(jnp.int32, (n_rows, n_pad), 0) % t_new
    s_idx = lax.broadcasted_iota(jnp.int32, (n_rows, n_pad), 1)
    a, mass = _sb_weights(_dot_nt(q_bd, kn_ref[...]), _upper(n_pad),
                          jnp.zeros((n_rows, 1), F32), s_idx < q_idx)
    acc_ref[...] = jnp.dot(a, vn_ref[...], preferred_element_type=F32)
    upper2 = _upper(tk)

    def sweep(slot, mass):
        z = jnp.dot(q_bd, k_buf[slot].astype(BF16), preferred_element_type=F32)
        a, mass = _sb_weights(z, upper2, mass, None)
        acc_ref[...] += _dot_nt(a, v_buf[slot].astype(BF16))
        return mass

    own = lax.rem(b, 2)
    arrive(b, last, own)
    mass = sweep(own, mass)

    def cond(state):
        j, _, least = state
        return jnp.logical_and(j >= 0, least < SB_DEAD_MASS)

    def body(state):
        j, mass, _ = state
        fetch(b, j, 2)
        arrive(b, j, 2)
        mass = sweep(2, mass)
        return j - 1, mass, jnp.min(mass)

    lax.while_loop(cond, body, (last - 1, mass, jnp.min(mass)))

    out_head = lax.broadcasted_iota(jnp.int32, (t_new, SB_WIDTH), 1) // HEAD_DIM
    out = jnp.zeros((t_new, SB_WIDTH), F32)
    for h in range(N_SB_HEADS):
        out = out + jnp.where(out_head == h, acc_ref[h * t_new:(h + 1) * t_new, :], 0.0)
    o_ref[...] = out.astype(o_ref.dtype)


def _sb_decode(q, k_new, v_new, cache_kt, cache_vt, *, batch, t_new):
    past = cache_kt.shape[2]
    tk = SB_TK
    assert past % tk == 0
    pad = ((0, 0), (0, LANES - t_new), (0, 0))
    kn = jnp.pad(k_new.reshape(batch, t_new, SB_WIDTH), pad)
    vn = jnp.pad(v_new.reshape(batch, t_new, SB_WIDTH), pad)
    row_spec = pl.BlockSpec((None, t_new, SB_WIDTH), lambda b: (b, 0, 0))
    new_spec = pl.BlockSpec((None, LANES, SB_WIDTH), lambda b: (b, 0, 0))
    hbm_spec = pl.BlockSpec(memory_space=pl.ANY)
    out = pl.pallas_call(
        functools.partial(_sb_decode_kernel, t_new=t_new, tk=tk, n_blocks=past // tk),
        out_shape=jax.ShapeDtypeStruct((batch, t_new, SB_WIDTH), BF16),
        grid=(batch,),
        in_specs=[row_spec, new_spec, new_spec, hbm_spec, hbm_spec],
        out_specs=row_spec,
        scratch_shapes=[pltpu.VMEM((3, SB_WIDTH, tk), F32),
                        pltpu.VMEM((3, SB_WIDTH, tk), F32),
                        pltpu.SemaphoreType.DMA((2, 3)),
                        pltpu.VMEM((N_SB_HEADS * t_new, SB_WIDTH), F32)],
        compiler_params=_cparams("arbitrary"),
        name="sb_decode",
    )(q.reshape(batch, t_new, SB_WIDTH), kn, vn, cache_kt, cache_vt)
    return out.reshape(batch * t_new, SB_WIDTH)


def _heads_major(x):
    b, t, h, d = x.shape
    return jnp.transpose(x, (0, 2, 3, 1)).reshape(b, h * d, t)


def _heads_minor(xt, heads):
    b, hd, t = xt.shape
    return jnp.transpose(xt.reshape(b, heads, hd // heads, t), (0, 3, 1, 2))


def _sgu_kernel(u_ref, v_ref, w_ref, b_ref, o_ref, *, span, n_chunks):
    lane = lax.broadcasted_iota(jnp.int32, (1, LANES), 1)
    group0 = lane < HEAD_DIM
    t = lax.broadcasted_iota(jnp.int32, (span, span), 0)
    s = lax.broadcasted_iota(jnp.int32, (span, span), 1)
    tril = s <= t
    for p in range(SGU_WIDTH // LANES):
        w_pair = jnp.concatenate(
            [jnp.where(tril, w_ref[2 * p], 0.0), jnp.where(tril, w_ref[2 * p + 1], 0.0)],
            axis=1).astype(BF16)
        cols = slice(p * LANES, (p + 1) * LANES)
        for c in range(n_chunks):
            rows = slice(c * span, (c + 1) * span)
            vv = v_ref[rows, cols]
            zv = jnp.zeros_like(vv)
            v_pair = jnp.concatenate([jnp.where(group0, vv, zv),
                                      jnp.where(group0, zv, vv)], axis=0)
            mixed = jnp.dot(w_pair, v_pair, preferred_element_type=F32) + b_ref[:, cols]
            o_ref[rows, cols] = (u_ref[rows, cols] * mixed).astype(o_ref.dtype)


def _sgu(u, v, w, bias, *, tm, name):
    m = u.shape[0]
    span = w.shape[-1]
    blk = pl.BlockSpec((tm, SGU_WIDTH), lambda i: (i, 0))
    return pl.pallas_call(
        functools.partial(_sgu_kernel, span=span, n_chunks=tm // span),
        out_shape=jax.ShapeDtypeStruct((m, SGU_WIDTH), BF16),
        grid=(m // tm,),
        in_specs=[blk, blk, _const_spec(w.shape), _const_spec(bias.shape)],
        out_specs=blk, compiler_params=_cparams("arbitrary"), name=name,
    )(u, v, w, bias)


_OUT_CHUNKS = [slice(c, c + OUT_CHUNK) for c in range(0, D_MODEL, OUT_CHUNK)]


def _mem_scores(qm_ref, mk_ref):
    lane = lax.broadcasted_iota(jnp.int32, (1, MEM_WIDTH), 1) // HEAD_DIM
    qm = qm_ref[...] * jnp.asarray(SB_SCALE, BF16)
    zq = jnp.zeros_like(qm)
    mk = mk_ref[...].astype(BF16)
    return [jnp.dot(jnp.where(lane == h, qm, zq), mk, preferred_element_type=F32)
            for h in range(N_MEM_HEADS)]


def _mem_rows(scores, mv_ref):
    sublane = lax.broadcasted_iota(jnp.int32, (MEM_WIDTH, 1), 0) // HEAD_DIM
    mv = mv_ref[...].astype(BF16)
    zv = jnp.zeros_like(mv)
    mem = 0.0
    for h, s in enumerate(scores):
        e = jnp.exp(s - jnp.max(s, axis=-1, keepdims=True))
        p = e / jnp.sum(e, axis=-1, keepdims=True)
        mem = mem + _dot_nt(p.astype(BF16), jnp.where(sublane == h, mv, zv))
    return mem.astype(BF16)


def _mixer_part(o, w_ref):
    return [jnp.dot(o, w_ref[:o.shape[1], cols], preferred_element_type=F32)
            for cols in _OUT_CHUNKS]


def _out_rows(y_ref, mixer_part, mem, w_ref):
    width = D_MODEL - mem.shape[1]
    return jnp.concatenate(
        [(y_ref[:, cols] + part) + jnp.dot(mem, w_ref[width:, cols],
                                           preferred_element_type=F32)
         for cols, part in zip(_OUT_CHUNKS, mixer_part)], axis=1)


def _mix_rows(y_ref, o_ref, qm_ref, mk_ref, mv_ref, w_ref):
    scores = _mem_scores(qm_ref, mk_ref)
    mixer_part = _mixer_part(o_ref[...], w_ref)
    return _out_rows(y_ref, mixer_part, _mem_rows(scores, mv_ref), w_ref)


def _gate_mix_kernel(y_ref, g_ref, w_ref, gn_ref, wsp_ref, b_ref, mk_ref, mv_ref,
                     wout_ref, out_ref, o_scr, *, span):
    h = _rmsnorm_rows(y_ref[...], g_ref[...]).astype(BF16)

    def proj(lo, hi):
        return [jnp.dot(h, w_ref[:, a:min(a + PROJ_CHUNK, hi)], preferred_element_type=F32)
                for a in range(lo, hi, PROJ_CHUNK)]

    acts = [jax.nn.gelu(p) for p in proj(SGU_WIDTH, 2 * SGU_WIDTH)]
    ms = sum(jnp.sum(t * t, axis=-1, keepdims=True) for t in acts) / SGU_WIDTH
    v = (jnp.concatenate(acts, axis=1) * lax.rsqrt(ms + EPS)) * gn_ref[...]
    u = jnp.concatenate([jax.nn.gelu(p) for p in proj(0, SGU_WIDTH)], axis=1)
    (qm,) = proj(2 * SGU_WIDTH, 2 * SGU_WIDTH + MEM_WIDTH)
    _sgu_kernel(u, v.astype(BF16), wsp_ref, b_ref, o_scr, span=span,
                n_chunks=y_ref.shape[0] // span)
    out_ref[...] = _mix_rows(y_ref, o_scr, qm.astype(BF16), mk_ref, mv_ref, wout_ref)


def _gate_mix(y, g, w_in, gate_norm, w_sp, bias, mem_kt, mem_vt, w_out, *,
              tm, rows_per_batch, name):
    m = y.shape[0]
    span = w_sp.shape[-1]
    assert rows_per_batch % tm == 0 and tm % span == 0
    per = rows_per_batch // tm
    row = pl.BlockSpec((tm, D_MODEL), lambda i: (i, 0))
    mem_spec = pl.BlockSpec((None, MEM_WIDTH, N_MEM), lambda i: (i // per, 0, 0))
    layered = [g, w_in, gate_norm]
    return pl.pallas_call(
        functools.partial(_gate_mix_kernel, span=span),
        out_shape=jax.ShapeDtypeStruct((m, D_MODEL), F32),
        grid=(m // tm,),
        in_specs=[row] + [_layer_spec(*pair) for pair in layered] + [
            _const_spec(w_sp.shape), _const_spec(bias.shape), mem_spec, mem_spec,
            _layer_spec(*w_out)],
        out_specs=row,
        scratch_shapes=[pltpu.VMEM((tm, SGU_WIDTH), BF16)],
        compiler_params=_cparams("arbitrary"), name=name,
    )(y, *[arr for arr, _ in layered], w_sp, bias, mem_kt, mem_vt, w_out[0])


def _ffn_rows(y, g_ref, wg_ref, wu_ref, wd_ref, gf_ref):
    h = _rmsnorm_rows(y, g_ref[...]).astype(BF16)
    acc = y
    for c in range(D_FF // FF_CHUNK):
        cols = slice(c * FF_CHUNK, (c + 1) * FF_CHUNK)
        gate = jnp.dot(h, wg_ref[:, cols], preferred_element_type=F32)
        up = jnp.dot(h, wu_ref[:, cols], preferred_element_type=F32)
        act = (jax.nn.silu(gate) * up).astype(BF16)
        acc = acc + jnp.dot(act, wd_ref[cols, :], preferred_element_type=F32)
    return acc if gf_ref is None else _rmsnorm_rows(acc, gf_ref[...])


def _ffn_stream_kernel(y_ref, g_ref, wg_ref, wu_ref, wd_ref, *rest, final):
    gf_ref = rest[0] if final else None
    out_ref, h_scr, acc_scr = rest[-3:]
    c = pl.program_id(0)

    @pl.when(c == 0)
    def _():
        y = y_ref[...]
        h_scr[...] = _rmsnorm_rows(y, g_ref[...]).astype(BF16)
        acc_scr[...] = y

    h = h_scr[...]
    gate = jnp.dot(h, wg_ref[...], preferred_element_type=F32)
    up = jnp.dot(h, wu_ref[...], preferred_element_type=F32)
    act = (jax.nn.silu(gate) * up).astype(BF16)
    acc_scr[...] += jnp.dot(act, wd_ref[...], preferred_element_type=F32)

    @pl.when(c == pl.num_programs(0) - 1)
    def _():
        acc = acc_scr[...]
        out_ref[...] = acc if gf_ref is None else _rmsnorm_rows(acc, gf_ref[...])


def _ffn_stream(y, g, w_gate, w_up, w_down, layer, *, g_final=None, name):
    m = y.shape[0]
    whole = pl.BlockSpec((m, D_MODEL), lambda c: (0, 0))
    in_specs = [whole, _layer_spec(g, layer),
                pl.BlockSpec((None, D_MODEL, FF_CHUNK), lambda c: (layer, 0, c)),
                pl.BlockSpec((None, D_MODEL, FF_CHUNK), lambda c: (layer, 0, c)),
                pl.BlockSpec((None, FF_CHUNK, D_MODEL), lambda c: (layer, c, 0))]
    args = [y, g, w_gate, w_up, w_down]
    if g_final is not None:
        in_specs.append(_const_spec((1, D_MODEL)))
        args.append(g_final.reshape(1, D_MODEL))
    return pl.pallas_call(
        functools.partial(_ffn_stream_kernel, final=g_final is not None),
        out_shape=jax.ShapeDtypeStruct((m, D_MODEL), F32),
        grid=(D_FF // FF_CHUNK,), in_specs=in_specs, out_specs=whole,
        scratch_shapes=[pltpu.VMEM((m, D_MODEL), BF16), pltpu.VMEM((m, D_MODEL), F32)],
        compiler_params=_cparams("arbitrary"), name=name,
    )(*args)


def _mix_ffn_kernel(*refs, mix, ffn, final):
    refs = list(refs)
    out_ref = refs.pop()
    if mix:
        y = _mix_rows(*refs[:6])
        del refs[:6]
    else:
        y = refs.pop(0)[...]
    if ffn:
        y = _ffn_rows(y, *refs[:4], refs[4] if final else None)
    out_ref[...] = y


def _mix_ffn(y, *, mix=None, ffn=None, g_final=None, tm, name):
    m = y.shape[0]
    row = pl.BlockSpec((tm, D_MODEL), lambda i: (i, 0))
    args, in_specs = [y], [row]
    if mix is not None:
        o, qm, mem_kt, mem_vt, w_out, layer, rows_per_batch = mix
        assert rows_per_batch % tm == 0
        per = rows_per_batch // tm
        mem_spec = pl.BlockSpec((None, MEM_WIDTH, N_MEM), lambda i: (i // per, 0, 0))
        args += [o, qm, mem_kt, mem_vt, w_out]
        in_specs += [pl.BlockSpec((tm, SB_WIDTH), lambda i: (i, 0)),
                     pl.BlockSpec((tm, MEM_WIDTH), lambda i: (i, 0)),
                     mem_spec, mem_spec, _layer_spec(w_out, layer)]
    if ffn is not None:
        *stacked, layer = ffn
        args += stacked
        in_specs += [_layer_spec(a, layer) for a in stacked]
        if g_final is not None:
            args.append(g_final.reshape(1, D_MODEL))
            in_specs.append(_const_spec((1, D_MODEL)))
    return pl.pallas_call(
        functools.partial(_mix_ffn_kernel, mix=mix is not None, ffn=ffn is not None,
                          final=g_final is not None),
        out_shape=jax.ShapeDtypeStruct((m, D_MODEL), F32),
        grid=(m // tm,), in_specs=in_specs, out_specs=row,
        compiler_params=_cparams("arbitrary"), name=name,
    )(*args)


_PLAN_A_PROMPT = ((0, SB_WIDTH, "bf16"), (SB_WIDTH, 2 * SB_WIDTH, "f32T+bf16"),
                  (2 * SB_WIDTH, 3 * SB_WIDTH, "f32T+bf16"),
                  (3 * SB_WIDTH, 3 * SB_WIDTH + MEM_WIDTH, "bf16"))
_PLAN_A_SAMPLE = ((0, SB_WIDTH, "f32"), (SB_WIDTH, 2 * SB_WIDTH, "f32+bf16"),
                  (2 * SB_WIDTH, 3 * SB_WIDTH, "f32+bf16"),
                  (3 * SB_WIDTH, 3 * SB_WIDTH + MEM_WIDTH, "bf16"))
_PLAN_B_PROMPT = ((SGU_WIDTH, 2 * SGU_WIDTH, "gelu_norm_b"), (0, SGU_WIDTH, "gelu"),
                  (2 * SGU_WIDTH, 2 * SGU_WIDTH + MEM_WIDTH, "bf16"))
_PLAN_B_SAMPLE = ((SGU_WIDTH, 2 * SGU_WIDTH, "gelu_norm"),) + _PLAN_B_PROMPT[1:]
_PLAN_MEM = ((0, MEM_WIDTH, "f32T"), (MEM_WIDTH, 2 * MEM_WIDTH, "f32T"))


def kernel(x_prompt, x_sample, cache_sb_k, cache_sb_v, cache_mem_k, cache_mem_v,
           mem_prompt, g_mix, w_in_a, w_in_b, w_sp, b_sp, g_sgu, g_mem, w_mem_kv,
           w_out, g_ffn, w_gate, w_up, w_down, g_final):
    batch, seq, _ = x_prompt.shape
    dec_batch, dec_seq, _ = x_sample.shape
    depth = g_mix.shape[0]
    tm_p = 1024
    tm_gate = 1024
    tm_s = dec_batch * dec_seq

    y_p = x_prompt.reshape(batch * seq, D_MODEL)
    y_s = x_sample.reshape(tm_s, D_MODEL)
    mem_rows = mem_prompt.reshape(batch * N_MEM, D_MODEL)

    g_mix, g_mem, g_ffn, g_sgu = (_as_rows(g) for g in (g_mix, g_mem, g_ffn, g_sgu))
    w_in_a, w_in_b, w_mem_kv, w_out, w_gate, w_up, w_down = (
        w.astype(BF16) for w in (w_in_a, w_in_b, w_mem_kv, w_out, w_gate, w_up, w_down))

    sb_k_p, sb_v_p, sb_k_s, sb_v_s, mem_k_p, mem_v_p, sgu_v_s = ([] for _ in range(7))
    for l in range(depth):
        mkt_p, mvt_p = _norm_proj(mem_rows, (g_mem, l), (w_mem_kv, l), _PLAN_MEM,
                                  tm=N_MEM, rows_per_batch=N_MEM, name=f"mem_kv{l}")
        mem_k_p.append(_heads_minor(mkt_p, N_MEM_HEADS))
        mem_v_p.append(_heads_minor(mvt_p, N_MEM_HEADS))
        mkt_s, mvt_s = _heads_major(cache_mem_k[l]), _heads_major(cache_mem_v[l])

        if l % 2 == 0:
            ia = l // 2
            q, kt_f, k_b, vt_f, v_b, qm_p = _norm_proj(
                y_p, (g_mix, l), (w_in_a, ia), _PLAN_A_PROMPT, tm=tm_p,
                rows_per_batch=seq, name=f"in_a{l}_p")
            sb_k_p.append(_heads_minor(kt_f, N_SB_HEADS))
            sb_v_p.append(_heads_minor(vt_f, N_SB_HEADS))
            y_p = _sb_prompt_mix(y_p, q, k_b, v_b, qm_p, mkt_p, mvt_p, (w_out, l),
                                 batch=batch, seq=seq, name=f"sb_mix{l}_p")

            q, k_f, k_b, v_f, v_b, qm_s = _norm_proj(
                y_s, (g_mix, l), (w_in_a, ia), _PLAN_A_SAMPLE, tm=tm_s, name=f"in_a{l}_s")
            sb_k_s.append(k_f.reshape(dec_batch, dec_seq, N_SB_HEADS, HEAD_DIM))
            sb_v_s.append(v_f.reshape(dec_batch, dec_seq, N_SB_HEADS, HEAD_DIM))
            o_s = _sb_decode(q, k_b, v_b, _heads_major(cache_sb_k[ia]),
                             _heads_major(cache_sb_v[ia]),
                             batch=dec_batch, t_new=dec_seq)
        else:
            ib = l // 2
            span = min(seq, MLP_CHUNK)
            bias = jnp.repeat(b_sp[ib][:, :span].T, HEAD_DIM, axis=1)
            y_p = _gate_mix(y_p, (g_mix, l), (w_in_b, ib), (g_sgu, ib),
                            w_sp[ib][:, :span, :span], bias, mkt_p, mvt_p, (w_out, l),
                            tm=tm_gate, rows_per_batch=seq, name=f"gate_mix{l}_p")

            v_f, v_b, u, qm_s = _norm_proj(
                y_s, (g_mix, l), (w_in_b, ib), _PLAN_B_SAMPLE, tm=tm_s,
                gate_norm=(g_sgu, ib), name=f"in_b{l}_s")
            sgu_v_s.append(v_f.reshape(dec_batch, dec_seq, SGU_WIDTH))
            span = min(dec_seq, MLP_CHUNK)
            reps = MLP_CHUNK // span
            eye = jnp.eye(reps, dtype=F32)
            w_bd = jnp.einsum("ab,gts->gatbs", eye, w_sp[ib][:, :span, :span]).reshape(
                -1, MLP_CHUNK, MLP_CHUNK)
            bias = jnp.tile(jnp.repeat(b_sp[ib][:, :span].T, HEAD_DIM, axis=1), (reps, 1))
            o_s = _sgu(u, v_b, w_bd, bias, tm=tm_s, name=f"sgu{l}_s")

        last = g_final if l == depth - 1 else None
        ffn = (g_ffn, w_gate, w_up, w_down, l)
        y_p = _mix_ffn(y_p, ffn=ffn, g_final=last, tm=tm_p, name=f"ffn{l}_p")
        y_s = _mix_ffn(y_s, mix=(o_s, qm_s, mkt_s, mvt_s, w_out, l, dec_seq),
                       tm=dec_seq, name=f"mix_out{l}_s")
        y_s = _ffn_stream(y_s, *ffn, g_final=last, name=f"ffn{l}_s")

    return (y_p.reshape(batch, seq, D_MODEL), y_s.reshape(dec_batch, dec_seq, D_MODEL),
            jnp.stack(sb_k_p), jnp.stack(sb_v_p), jnp.stack(sb_k_s), jnp.stack(sb_v_s),
            jnp.stack(mem_k_p), jnp.stack(mem_v_p), jnp.stack(sgu_v_s))
```

```python
import functools

import jax
import jax.numpy as jnp
from jax import lax
from jax.experimental import pallas as pl
from jax.experimental.pallas import tpu as pltpu

F32 = jnp.float32
BF16 = jnp.bfloat16

D_MODEL = 1024
HEAD_DIM = 64
SB_WIDTH = 768
N_SB_HEADS = SB_WIDTH // HEAD_DIM
SGU_WIDTH = 768
MLP_CHUNK = 128
MEM_WIDTH = 256
N_MEM = 256
N_MEM_HEADS = MEM_WIDTH // HEAD_DIM
D_FF = 2816
EPS = 1e-6
SB_SCALE = HEAD_DIM ** -0.5
LOG2E = 1.4426950408889634

LANES = 128
V7X_VMEM_BYTES = 64 * 1024 * 1024
VMEM_LIMIT = V7X_VMEM_BYTES - 8 * 1024 * 1024

SB_DEAD_MASS = 106.0

SB_TQ = 256
SB_TK = 256
SB_EARLY = 160
FF_CHUNK = 256
OUT_CHUNK = 256
OUT_DEPTH = 256
PROJ_CHUNK = 256


def _cparams(*semantics):
    return pltpu.CompilerParams(dimension_semantics=semantics,
                                vmem_limit_bytes=VMEM_LIMIT)


def _const_spec(shape):
    zeros = (0,) * len(shape)
    return pl.BlockSpec(shape, lambda *_: zeros, pipeline_mode=pl.Buffered(1))


def _layer_spec(stacked, layer):
    return pl.BlockSpec((None,) + stacked.shape[1:], lambda *_: (layer, 0, 0),
                        pipeline_mode=pl.Buffered(1))


def _as_rows(g):
    return g.reshape(g.shape[0], 1, g.shape[1])


def _rmsnorm_rows(x, g):
    ms = jnp.mean(x * x, axis=-1, keepdims=True)
    return (x * lax.rsqrt(ms + EPS)) * g


def _dot_nt(a, b):
    return lax.dot_general(a, b, (((1,), (1,)), ((), ())), preferred_element_type=F32)


_PLAN_FORMS = {
    "bf16": ("bf16",),
    "f32": ("f32",),
    "f32+bf16": ("f32", "bf16"),
    "f32T": ("f32T",),
    "f32T+bf16": ("f32T", "bf16"),
    "gelu": ("f32",),
    "gelu_norm": ("f32", "bf16"),
    "gelu_norm_b": ("bf16",),
}


def _norm_proj_kernel(x_ref, g_ref, w_ref, *rest, plan, has_gate_norm):
    if has_gate_norm:
        gn_ref, out_refs = rest[0], rest[1:]
    else:
        gn_ref, out_refs = None, rest
    h = _rmsnorm_rows(x_ref[...], g_ref[...]).astype(BF16)
    tasks, k = [], 0
    for lo, hi, kind in plan:
        refs = out_refs[k:k + len(_PLAN_FORMS[kind])]
        k += len(refs)
        tasks += [(lo, hi, kind, refs, c, min(c + PROJ_CHUNK, hi))
                  for c in range(lo, hi, PROJ_CHUNK)]

    def product(task):
        a, b = task[4:]
        return jnp.dot(h, w_ref[:, a:b], preferred_element_type=F32)

    def store(refs, forms, a, b, value):
        for ref, form in zip(refs, forms):
            if form == "f32T":
                ref[a:b, :] = value.T
            else:
                ref[:, a:b] = value.astype(BF16 if form == "bf16" else F32)

    acts = []
    ahead = product(tasks[0])
    for n, task in enumerate(tasks):
        lo, hi, kind, refs, a, b = task
        p = ahead
        if n + 1 < len(tasks):
            ahead = product(tasks[n + 1])
        if kind.startswith("gelu"):
            p = jax.nn.gelu(p)
        if not kind.startswith("gelu_norm"):
            store(refs, _PLAN_FORMS[kind], a - lo, b - lo, p)
            continue
        acts.append((a - lo, b - lo, p))
        if b == hi:
            ms = sum(jnp.sum(t * t, axis=-1, keepdims=True) for _, _, t in acts) / (hi - lo)
            inv = lax.rsqrt(ms + EPS)
            for a0, b0, t in acts:
                store(refs, _PLAN_FORMS[kind], a0, b0, (t * inv) * gn_ref[:, a0:b0])
            acts = []


def _norm_proj(x, g, w, plan, *, tm, rows_per_batch=None, gate_norm=None, name):
    m, d = x.shape
    out_shape, out_specs = [], []
    for lo, hi, kind in plan:
        for form in _PLAN_FORMS[kind]:
            if form == "f32T":
                per = rows_per_batch // tm
                out_shape.append(jax.ShapeDtypeStruct(
                    (m // rows_per_batch, hi - lo, rows_per_batch), F32))
                out_specs.append(pl.BlockSpec(
                    (None, hi - lo, tm), lambda i, per=per: (i // per, 0, i % per)))
            else:
                out_shape.append(jax.ShapeDtypeStruct(
                    (m, hi - lo), BF16 if form == "bf16" else F32))
                out_specs.append(pl.BlockSpec((tm, hi - lo), lambda i: (i, 0)))
    layered = [g, w] + ([gate_norm] if gate_norm is not None else [])
    in_specs = [pl.BlockSpec((tm, d), lambda i: (i, 0))] + [
        _layer_spec(arr, layer) for arr, layer in layered]
    args = [x] + [arr for arr, _ in layered]
    return pl.pallas_call(
        functools.partial(_norm_proj_kernel, plan=plan,
                          has_gate_norm=gate_norm is not None),
        out_shape=out_shape, grid=(m // tm,), in_specs=in_specs,
        out_specs=out_specs, compiler_params=_cparams("arbitrary"), name=name,
    )(*args)


def _upper(n):
    j = lax.broadcasted_iota(jnp.int32, (n, n), 0)
    s = lax.broadcasted_iota(jnp.int32, (n, n), 1)
    return jnp.where(j >= s, 1.0, 0.0).astype(BF16)


def _sb_weights(z, upper, mass, causal):
    soft = jnp.maximum(z, 0.0) + jnp.log(1.0 + jnp.exp2(jnp.abs(z) * -LOG2E))
    if causal is not None:
        soft = jnp.where(causal, soft, 0.0)
    from_here = jnp.dot(soft.astype(BF16), upper, preferred_element_type=F32) + mass
    a = jnp.exp2((z - from_here) * LOG2E)
    if causal is not None:
        a = jnp.where(causal, a, 0.0)
    return a.astype(BF16), mass + jnp.sum(soft, axis=-1, keepdims=True)


def _sb_prompt_kernel(q_ref, k_ref, v_ref, y_ref, qm_ref, mk_ref, mv_ref, w_ref,
                      out_ref, mass_ref, *, tq, tk, early):
    i = pl.program_id(1)
    pairs = SB_WIDTH // LANES
    lane = lax.broadcasted_iota(jnp.int32, (1, LANES), 1)
    head0 = lane < HEAD_DIM
    upper2 = _upper(tk)
    mem = _mem_rows(_mem_scores(qm_ref, mk_ref), mv_ref)
    row = lax.broadcasted_iota(jnp.int32, (2 * tq, tk), 0) % tq
    col = lax.broadcasted_iota(jnp.int32, (2 * tq, tk), 1)
    has_prev = i > 0
    j_prev = jnp.maximum(i - 1, 0)

    def both_heads(x, lo, hi):
        return jnp.concatenate([x[lo:hi], x[tq + lo:tq + hi]], axis=0)

    def sweeper(p):
        lanes = slice(p * LANES, (p + 1) * LANES)
        q = q_ref[:, lanes] * jnp.asarray(SB_SCALE, BF16)
        zq = jnp.zeros_like(q)
        q2 = jnp.concatenate([jnp.where(head0, q, zq), jnp.where(head0, zq, q)], axis=0)

        def logits(qs, j):
            return _dot_nt(qs, k_ref[pl.ds(pl.multiple_of(j * tk, tk), tk), lanes])

        def values(a, j):
            n = a.shape[0] // 2
            vb = v_ref[pl.ds(pl.multiple_of(j * tk, tk), tk), lanes]
            zv = jnp.zeros_like(vb)
            return (jnp.dot(a[:n], jnp.where(head0, vb, zv), preferred_element_type=F32)
                    + jnp.dot(a[n:], jnp.where(head0, zv, vb), preferred_element_type=F32))

        return q2, logits, values

    late = tq - early
    least_early = least_late = jnp.asarray(jnp.inf, F32)
    mixed, mixer_part = [], None
    for p in range(pairs):
        q2, logits, values = sweeper(p)
        z_diag, z_prev = logits(q2, i), logits(both_heads(q2, 0, early), j_prev)
        a_diag, mass_diag = _sb_weights(z_diag, upper2, jnp.zeros((2 * tq, 1), F32),
                                        col < row)
        mass_e = both_heads(mass_diag, 0, early)
        a_prev, mass_prev = _sb_weights(z_prev, upper2, mass_e, None)
        acc = values(a_diag, i)
        acc_early = acc[:early] + jnp.where(has_prev, values(a_prev, j_prev), 0.0)
        mixed.append(jnp.concatenate([acc_early, acc[early:]], axis=0).astype(BF16))
        if len(mixed) * LANES == OUT_DEPTH:
            rows = slice((p + 1) * LANES - OUT_DEPTH, (p + 1) * LANES)
            share = [jnp.dot(jnp.concatenate(mixed, axis=1), w_ref[rows, cols],
                             preferred_element_type=F32) for cols in _OUT_CHUNKS]
            mixer_part = share if mixer_part is None else [
                s + t for s, t in zip(mixer_part, share)]
            mixed = []
        mass_e = jnp.where(has_prev, mass_prev, mass_e)
        mass_l = both_heads(mass_diag, early, tq)
        mass_ref[p] = jnp.concatenate(
            [mass_e[:early], mass_l[:late], mass_e[early:], mass_l[late:]], axis=0)
        least_early = jnp.minimum(least_early, jnp.min(mass_e))
        least_late = jnp.minimum(least_late, jnp.min(mass_l))
    out_ref[...] = _out_rows(y_ref, mixer_part, mem, w_ref)

    late_row = lax.broadcasted_iota(jnp.int32, (2 * tq, 1), 0) % tq >= early

    def cond(state):
        j, least = state
        return jnp.logical_and(j >= 0, least < SB_DEAD_MASS)

    def body(state):
        j, _ = state
        fresh = jnp.logical_or(late_row, j != i - 1)
        least = jnp.asarray(jnp.inf, F32)
        for p in range(pairs):
            q2, logits, values = sweeper(p)
            mass_old = mass_ref[p]
            a, mass = _sb_weights(logits(q2, j), upper2, mass_old, None)
            a = jnp.where(fresh, a, jnp.zeros_like(a))
            mass = jnp.where(fresh, mass, mass_old)
            out_ref[...] += jnp.dot(values(a, j).astype(BF16),
                                    w_ref[p * LANES:(p + 1) * LANES, :],
                                    preferred_element_type=F32)
            mass_ref[p] = mass
            least = jnp.minimum(least, jnp.min(mass))
        return j - 1, least

    j_start = jnp.where(least_late < SB_DEAD_MASS, i - 1, i - 2)
    lax.while_loop(cond, body, (j_start, jnp.minimum(least_early, least_late)))


def _sb_prompt_mix(y, q, k, v, qm, mem_kt, mem_vt, w_out, *, batch, seq, name):
    tq, tk = SB_TQ, SB_TK
    assert tq == tk and seq % tq == 0
    per = seq // tq
    q3, k3, v3 = (a.reshape(batch, seq, SB_WIDTH) for a in (q, k, v))
    q_spec = pl.BlockSpec((None, tq, SB_WIDTH), lambda b, i: (b, i, 0))
    kv_spec = pl.BlockSpec((None, seq, SB_WIDTH), lambda b, i: (b, 0, 0))
    row = lambda width: pl.BlockSpec((tq, width), lambda b, i: (b * per + i, 0))
    mem_spec = pl.BlockSpec((None, MEM_WIDTH, N_MEM), lambda b, i: (b, 0, 0))
    return pl.pallas_call(
        functools.partial(_sb_prompt_kernel, tq=tq, tk=tk, early=SB_EARLY),
        out_shape=jax.ShapeDtypeStruct((batch * seq, D_MODEL), F32),
        grid=(batch, per),
        in_specs=[q_spec, kv_spec, kv_spec, row(D_MODEL), row(MEM_WIDTH),
                  mem_spec, mem_spec, _layer_spec(*w_out)],
        out_specs=row(D_MODEL),
        scratch_shapes=[pltpu.VMEM((SB_WIDTH // LANES, 2 * tq, 1), F32)],
        compiler_params=_cparams("arbitrary", "arbitrary"),
        name=name,
    )(q3, k3, v3, y, qm, mem_kt, mem_vt, w_out[0])


def _sb_decode_kernel(q_ref, kn_ref, vn_ref, kc_hbm, vc_hbm, o_ref,
                      k_buf, v_buf, sems, acc_ref, *, t_new, tk, n_blocks):
    b = pl.program_id(0)
    n_rows = N_SB_HEADS * t_new
    last = n_blocks - 1

    def block_copies(stream, j, slot):
        keys = pl.ds(pl.multiple_of(j * tk, tk), tk)
        return (pltpu.make_async_copy(kc_hbm.at[stream, :, keys], k_buf.at[slot],
                                      sems.at[0, slot]),
                pltpu.make_async_copy(vc_hbm.at[stream, :, keys], v_buf.at[slot],
                                      sems.at[1, slot]))

    def fetch(stream, j, slot):
        for copy in block_copies(stream, j, slot):
            copy.start()

    def arrive(stream, j, slot):
        for copy in block_copies(stream, j, slot):
            copy.wait()

    @pl.when(b == 0)
    def _():
        fetch(b, last, 0)

    @pl.when(b + 1 < pl.num_programs(0))
    def _():
        fetch(b + 1, last, lax.rem(b + 1, 2))

    row_head = lax.broadcasted_iota(jnp.int32, (n_rows, SB_WIDTH), 0) // t_new
    col_head = lax.broadcasted_iota(jnp.int32, (n_rows, SB_WIDTH), 1) // HEAD_DIM
    q_rows = jnp.concatenate([q_ref[...]] * N_SB_HEADS, axis=0)
    q_bd = jnp.where(row_head == col_head, q_rows * SB_SCALE, 0.0).astype(BF16)

    n_pad = kn_ref.shape[0]
    q_idx = lax.broadcasted_iota(jnp.int32, (n_rows, n_pad), 0) % t_new
    s_idx = lax.broadcasted_iota(jnp.int32, (n_rows, n_pad), 1)
    a, mass = _sb_weights(_dot_nt(q_bd, kn_ref[...]), _upper(n_pad),
                          jnp.zeros((n_rows, 1), F32), s_idx < q_idx)
    acc_ref[...] = jnp.dot(a, vn_ref[...], preferred_element_type=F32)
    upper2 = _upper(tk)

    def sweep(slot, mass):
        z = jnp.dot(q_bd, k_buf[slot].astype(BF16), preferred_element_type=F32)
        a, mass = _sb_weights(z, upper2, mass, None)
        acc_ref[...] += _dot_nt(a, v_buf[slot].astype(BF16))
        return mass

    own = lax.rem(b, 2)
    arrive(b, last, own)
    mass = sweep(own, mass)

    def cond(state):
        j, _, least = state
        return jnp.logical_and(j >= 0, least < SB_DEAD_MASS)

    def body(state):
        j, mass, _ = state
        fetch(b, j, 2)
        arrive(b, j, 2)
        mass = sweep(2, mass)
        return j - 1, mass, jnp.min(mass)

    lax.while_loop(cond, body, (last - 1, mass, jnp.min(mass)))

    out_head = lax.broadcasted_iota(jnp.int32, (t_new, SB_WIDTH), 1) // HEAD_DIM
    out = jnp.zeros((t_new, SB_WIDTH), F32)
    for h in range(N_SB_HEADS):
        out = out + jnp.where(out_head == h, acc_ref[h * t_new:(h + 1) * t_new, :], 0.0)
    o_ref[...] = out.astype(o_ref.dtype)


def _sb_decode(q, k_new, v_new, cache_kt, cache_vt, *, batch, t_new):
    past = cache_kt.shape[2]
    tk = SB_TK
    assert past % tk == 0
    pad = ((0, 0), (0, LANES - t_new), (0, 0))
    kn = jnp.pad(k_new.reshape(batch, t_new, SB_WIDTH), pad)
    vn = jnp.pad(v_new.reshape(batch, t_new, SB_WIDTH), pad)
    row_spec = pl.BlockSpec((None, t_new, SB_WIDTH), lambda b: (b, 0, 0))
    new_spec = pl.BlockSpec((None, LANES, SB_WIDTH), lambda b: (b, 0, 0))
    hbm_spec = pl.BlockSpec(memory_space=pl.ANY)
    out = pl.pallas_call(
        functools.partial(_sb_decode_kernel, t_new=t_new, tk=tk, n_blocks=past // tk),
        out_shape=jax.ShapeDtypeStruct((batch, t_new, SB_WIDTH), BF16),
        grid=(batch,),
        in_specs=[row_spec, new_spec, new_spec, hbm_spec, hbm_spec],
        out_specs=row_spec,
        scratch_shapes=[pltpu.VMEM((3, SB_WIDTH, tk), F32),
                        pltpu.VMEM((3, SB_WIDTH, tk), F32),
                        pltpu.SemaphoreType.DMA((2, 3)),
                        pltpu.VMEM((N_SB_HEADS * t_new, SB_WIDTH), F32)],
        compiler_params=_cparams("arbitrary"),
        name="sb_decode",
    )(q.reshape(batch, t_new, SB_WIDTH), kn, vn, cache_kt, cache_vt)
    return out.reshape(batch * t_new, SB_WIDTH)


def _heads_major(x):
    b, t, h, d = x.shape
    return jnp.transpose(x, (0, 2, 3, 1)).reshape(b, h * d, t)


def _heads_minor(xt, heads):
    b, hd, t = xt.shape
    return jnp.transpose(xt.reshape(b, heads, hd // heads, t), (0, 3, 1, 2))


def _sgu_kernel(u_ref, v_ref, w_ref, b_ref, o_ref, *, span, n_chunks):
    lane = lax.broadcasted_iota(jnp.int32, (1, LANES), 1)
    group0 = lane < HEAD_DIM
    t = lax.broadcasted_iota(jnp.int32, (span, span), 0)
    s = lax.broadcasted_iota(jnp.int32, (span, span), 1)
    tril = s <= t
    for p in range(SGU_WIDTH // LANES):
        w_pair = jnp.concatenate(
            [jnp.where(tril, w_ref[2 * p], 0.0), jnp.where(tril, w_ref[2 * p + 1], 0.0)],
            axis=1).astype(BF16)
        cols = slice(p * LANES, (p + 1) * LANES)
        for c in range(n_chunks):
            rows = slice(c * span, (c + 1) * span)
            vv = v_ref[rows, cols]
            zv = jnp.zeros_like(vv)
            v_pair = jnp.concatenate([jnp.where(group0, vv, zv),
                                      jnp.where(group0, zv, vv)], axis=0)
            mixed = jnp.dot(w_pair, v_pair, preferred_element_type=F32) + b_ref[:, cols]
            o_ref[rows, cols] = (u_ref[rows, cols] * mixed).astype(o_ref.dtype)


def _sgu(u, v, w, bias, *, tm, name):
    m = u.shape[0]
    span = w.shape[-1]
    blk = pl.BlockSpec((tm, SGU_WIDTH), lambda i: (i, 0))
    return pl.pallas_call(
        functools.partial(_sgu_kernel, span=span, n_chunks=tm // span),
        out_shape=jax.ShapeDtypeStruct((m, SGU_WIDTH), BF16),
        grid=(m // tm,),
        in_specs=[blk, blk, _const_spec(w.shape), _const_spec(bias.shape)],
        out_specs=blk, compiler_params=_cparams("arbitrary"), name=name,
    )(u, v, w, bias)


_OUT_CHUNKS = [slice(c, c + OUT_CHUNK) for c in range(0, D_MODEL, OUT_CHUNK)]


def _mem_scores(qm_ref, mk_ref):
    lane = lax.broadcasted_iota(jnp.int32, (1, MEM_WIDTH), 1) // HEAD_DIM
    qm = qm_ref[...] * jnp.asarray(SB_SCALE, BF16)
    zq = jnp.zeros_like(qm)
    mk = mk_ref[...].astype(BF16)
    return [jnp.dot(jnp.where(lane == h, qm, zq), mk, preferred_element_type=F32)
            for h in range(N_MEM_HEADS)]


def _mem_rows(scores, mv_ref):
    sublane = lax.broadcasted_iota(jnp.int32, (MEM_WIDTH, 1), 0) // HEAD_DIM
    mv = mv_ref[...].astype(BF16)
    zv = jnp.zeros_like(mv)
    mem = 0.0
    for h, s in enumerate(scores):
        e = jnp.exp(s - jnp.max(s, axis=-1, keepdims=True))
        p = e / jnp.sum(e, axis=-1, keepdims=True)
        mem = mem + _dot_nt(p.astype(BF16), jnp.where(sublane == h, mv, zv))
    return mem.astype(BF16)


def _mixer_part(o, w_ref):
    return [jnp.dot(o, w_ref[:o.shape[1], cols], preferred_element_type=F32)
            for cols in _OUT_CHUNKS]


def _out_rows(y_ref, mixer_part, mem, w_ref):
    width = D_MODEL - mem.shape[1]
    return jnp.concatenate(
        [(y_ref[:, cols] + part) + jnp.dot(mem, w_ref[width:, cols],
                                           preferred_element_type=F32)
         for cols, part in zip(_OUT_CHUNKS, mixer_part)], axis=1)


def _mix_rows(y_ref, o_ref, qm_ref, mk_ref, mv_ref, w_ref):
    scores = _mem_scores(qm_ref, mk_ref)
    mixer_part = _mixer_part(o_ref[...], w_ref)
    return _out_rows(y_ref, mixer_part, _mem_rows(scores, mv_ref), w_ref)


def _gate_mix_kernel(y_ref, g_ref, w_ref, gn_ref, wsp_ref, b_ref, mk_ref, mv_ref,
                     wout_ref, out_ref, o_scr, *, span):
    h = _rmsnorm_rows(y_ref[...], g_ref[...]).astype(BF16)

    def proj(lo, hi):
        return [jnp.dot(h, w_ref[:, a:min(a + PROJ_CHUNK, hi)], preferred_element_type=F32)
                for a in range(lo, hi, PROJ_CHUNK)]

    acts = [jax.nn.gelu(p) for p in proj(SGU_WIDTH, 2 * SGU_WIDTH)]
    ms = sum(jnp.sum(t * t, axis=-1, keepdims=True) for t in acts) / SGU_WIDTH
    v = (jnp.concatenate(acts, axis=1) * lax.rsqrt(ms + EPS)) * gn_ref[...]
    u = jnp.concatenate([jax.nn.gelu(p) for p in proj(0, SGU_WIDTH)], axis=1)
    (qm,) = proj(2 * SGU_WIDTH, 2 * SGU_WIDTH + MEM_WIDTH)
    _sgu_kernel(u, v.astype(BF16), wsp_ref, b_ref, o_scr, span=span,
                n_chunks=y_ref.shape[0] // span)
    out_ref[...] = _mix_rows(y_ref, o_scr, qm.astype(BF16), mk_ref, mv_ref, wout_ref)


def _gate_mix(y, g, w_in, gate_norm, w_sp, bias, mem_kt, mem_vt, w_out, *,
              tm, rows_per_batch, name):
    m = y.shape[0]
    span = w_sp.shape[-1]
    assert rows_per_batch % tm == 0 and tm % span == 0
    per = rows_per_batch // tm
    row = pl.BlockSpec((tm, D_MODEL), lambda i: (i, 0))
    mem_spec = pl.BlockSpec((None, MEM_WIDTH, N_MEM), lambda i: (i // per, 0, 0))
    layered = [g, w_in, gate_norm]
    return pl.pallas_call(
        functools.partial(_gate_mix_kernel, span=span),
        out_shape=jax.ShapeDtypeStruct((m, D_MODEL), F32),
        grid=(m // tm,),
        in_specs=[row] + [_layer_spec(*pair) for pair in layered] + [
            _const_spec(w_sp.shape), _const_spec(bias.shape), mem_spec, mem_spec,
            _layer_spec(*w_out)],
        out_specs=row,
        scratch_shapes=[pltpu.VMEM((tm, SGU_WIDTH), BF16)],
        compiler_params=_cparams("arbitrary"), name=name,
    )(y, *[arr for arr, _ in layered], w_sp, bias, mem_kt, mem_vt, w_out[0])


def _ffn_rows(y, g_ref, wg_ref, wu_ref, wd_ref, gf_ref):
    h = _rmsnorm_rows(y, g_ref[...]).astype(BF16)
    acc = y
    for c in range(D_FF // FF_CHUNK):
        cols = slice(c * FF_CHUNK, (c + 1) * FF_CHUNK)
        gate = jnp.dot(h, wg_ref[:, cols], preferred_element_type=F32)
        up = jnp.dot(h, wu_ref[:, cols], preferred_element_type=F32)
        act = (jax.nn.silu(gate) * up).astype(BF16)
        acc = acc + jnp.dot(act, wd_ref[cols, :], preferred_element_type=F32)
    return acc if gf_ref is None else _rmsnorm_rows(acc, gf_ref[...])


def _mix_ffn_kernel(*refs, mix, ffn, final):
    refs = list(refs)
    out_ref = refs.pop()
    if mix:
        y = _mix_rows(*refs[:6])
        del refs[:6]
    else:
        y = refs.pop(0)[...]
    if ffn:
        y = _ffn_rows(y, *refs[:4], refs[4] if final else None)
    out_ref[...] = y


def _mix_ffn(y, *, mix=None, ffn=None, g_final=None, tm, name):
    m = y.shape[0]
    row = pl.BlockSpec((tm, D_MODEL), lambda i: (i, 0))
    args, in_specs = [y], [row]
    if mix is not None:
        o, qm, mem_kt, mem_vt, w_out, layer, rows_per_batch = mix
        assert rows_per_batch % tm == 0
        per = rows_per_batch // tm
        mem_spec = pl.BlockSpec((None, MEM_WIDTH, N_MEM), lambda i: (i // per, 0, 0))
        args += [o, qm, mem_kt, mem_vt, w_out]
        in_specs += [pl.BlockSpec((tm, SB_WIDTH), lambda i: (i, 0)),
                     pl.BlockSpec((tm, MEM_WIDTH), lambda i: (i, 0)),
                     mem_spec, mem_spec, _layer_spec(w_out, layer)]
    if ffn is not None:
        *stacked, layer = ffn
        args += stacked
        in_specs += [_layer_spec(a, layer) for a in stacked]
        if g_final is not None:
            args.append(g_final.reshape(1, D_MODEL))
            in_specs.append(_const_spec((1, D_MODEL)))
    return pl.pallas_call(
        functools.partial(_mix_ffn_kernel, mix=mix is not None, ffn=ffn is not None,
                          final=g_final is not None),
        out_shape=jax.ShapeDtypeStruct((m, D_MODEL), F32),
        grid=(m // tm,), in_specs=in_specs, out_specs=row,
        compiler_params=_cparams("arbitrary"), name=name,
    )(*args)


_PLAN_A_PROMPT = ((0, SB_WIDTH, "bf16"), (SB_WIDTH, 2 * SB_WIDTH, "f32T+bf16"),
                  (2 * SB_WIDTH, 3 * SB_WIDTH, "f32T+bf16"),
                  (3 * SB_WIDTH, 3 * SB_WIDTH + MEM_WIDTH, "bf16"))
_PLAN_A_SAMPLE = ((0, SB_WIDTH, "f32"), (SB_WIDTH, 2 * SB_WIDTH, "f32+bf16"),
                  (2 * SB_WIDTH, 3 * SB_WIDTH, "f32+bf16"),
                  (3 * SB_WIDTH, 3 * SB_WIDTH + MEM_WIDTH, "bf16"))
_PLAN_B_PROMPT = ((SGU_WIDTH, 2 * SGU_WIDTH, "gelu_norm_b"), (0, SGU_WIDTH, "gelu"),
                  (2 * SGU_WIDTH, 2 * SGU_WIDTH + MEM_WIDTH, "bf16"))
_PLAN_B_SAMPLE = ((SGU_WIDTH, 2 * SGU_WIDTH, "gelu_norm"),) + _PLAN_B_PROMPT[1:]
_PLAN_MEM = ((0, MEM_WIDTH, "f32T"), (MEM_WIDTH, 2 * MEM_WIDTH, "f32T"))


def kernel(x_prompt, x_sample, cache_sb_k, cache_sb_v, cache_mem_k, cache_mem_v,
           mem_prompt, g_mix, w_in_a, w_in_b, w_sp, b_sp, g_sgu, g_mem, w_mem_kv,
           w_out, g_ffn, w_gate, w_up, w_down, g_final):
    batch, seq, _ = x_prompt.shape
    dec_batch, dec_seq, _ = x_sample.shape
    depth = g_mix.shape[0]
    tm_p = 1024
    tm_gate = 1024
    tm_s = dec_batch * dec_seq

    y_p = x_prompt.reshape(batch * seq, D_MODEL)
    y_s = x_sample.reshape(tm_s, D_MODEL)
    mem_rows = mem_prompt.reshape(batch * N_MEM, D_MODEL)

    g_mix, g_mem, g_ffn, g_sgu = (_as_rows(g) for g in (g_mix, g_mem, g_ffn, g_sgu))
    w_in_a, w_in_b, w_mem_kv, w_out, w_gate, w_up, w_down = (
        w.astype(BF16) for w in (w_in_a, w_in_b, w_mem_kv, w_out, w_gate, w_up, w_down))

    sb_k_p, sb_v_p, sb_k_s, sb_v_s, mem_k_p, mem_v_p, sgu_v_s = ([] for _ in range(7))
    for l in range(depth):
        mkt_p, mvt_p = _norm_proj(mem_rows, (g_mem, l), (w_mem_kv, l), _PLAN_MEM,
                                  tm=N_MEM, rows_per_batch=N_MEM, name=f"mem_kv{l}")
        mem_k_p.append(_heads_minor(mkt_p, N_MEM_HEADS))
        mem_v_p.append(_heads_minor(mvt_p, N_MEM_HEADS))
        mkt_s, mvt_s = _heads_major(cache_mem_k[l]), _heads_major(cache_mem_v[l])

        if l % 2 == 0:
            ia = l // 2
            q, kt_f, k_b, vt_f, v_b, qm_p = _norm_proj(
                y_p, (g_mix, l), (w_in_a, ia), _PLAN_A_PROMPT, tm=tm_p,
                rows_per_batch=seq, name=f"in_a{l}_p")
            sb_k_p.append(_heads_minor(kt_f, N_SB_HEADS))
            sb_v_p.append(_heads_minor(vt_f, N_SB_HEADS))
            y_p = _sb_prompt_mix(y_p, q, k_b, v_b, qm_p, mkt_p, mvt_p, (w_out, l),
                                 batch=batch, seq=seq, name=f"sb_mix{l}_p")

            q, k_f, k_b, v_f, v_b, qm_s = _norm_proj(
                y_s, (g_mix, l), (w_in_a, ia), _PLAN_A_SAMPLE, tm=tm_s, name=f"in_a{l}_s")
            sb_k_s.append(k_f.reshape(dec_batch, dec_seq, N_SB_HEADS, HEAD_DIM))
            sb_v_s.append(v_f.reshape(dec_batch, dec_seq, N_SB_HEADS, HEAD_DIM))
            o_s = _sb_decode(q, k_b, v_b, _heads_major(cache_sb_k[ia]),
                             _heads_major(cache_sb_v[ia]),
                             batch=dec_batch, t_new=dec_seq)
        else:
            ib = l // 2
            span = min(seq, MLP_CHUNK)
            bias = jnp.repeat(b_sp[ib][:, :span].T, HEAD_DIM, axis=1)
            y_p = _gate_mix(y_p, (g_mix, l), (w_in_b, ib), (g_sgu, ib),
                            w_sp[ib][:, :span, :span], bias, mkt_p, mvt_p, (w_out, l),
                            tm=tm_gate, rows_per_batch=seq, name=f"gate_mix{l}_p")

            v_f, v_b, u, qm_s = _norm_proj(
                y_s, (g_mix, l), (w_in_b, ib), _PLAN_B_SAMPLE, tm=tm_s,
                gate_norm=(g_sgu, ib), name=f"in_b{l}_s")
            sgu_v_s.append(v_f.reshape(dec_batch, dec_seq, SGU_WIDTH))
            span = min(dec_seq, MLP_CHUNK)
            reps = MLP_CHUNK // span
            eye = jnp.eye(reps, dtype=F32)
            w_bd = jnp.einsum("ab,gts->gatbs", eye, w_sp[ib][:, :span, :span]).reshape(
                -1, MLP_CHUNK, MLP_CHUNK)
            bias = jnp.tile(jnp.repeat(b_sp[ib][:, :span].T, HEAD_DIM, axis=1), (reps, 1))
            o_s = _sgu(u, v_b, w_bd, bias, tm=tm_s, name=f"sgu{l}_s")

        last = g_final if l == depth - 1 else None
        ffn = (g_ffn, w_gate, w_up, w_down, l)
        y_p = _mix_ffn(y_p, ffn=ffn, g_final=last, tm=tm_p, name=f"ffn{l}_p")
        y_s = _mix_ffn(y_s, mix=(o_s, qm_s, mkt_s, mvt_s, w_out, l, dec_seq),
                       tm=dec_seq, name=f"mix_out{l}_s")
        y_s = _mix_ffn(y_s, ffn=ffn, g_final=last, tm=tm_s, name=f"ffn{l}_s")

    return (y_p.reshape(batch, seq, D_MODEL), y_s.reshape(dec_batch, dec_seq, D_MODEL),
            jnp.stack(sb_k_p), jnp.stack(sb_v_p), jnp.stack(sb_k_s), jnp.stack(sb_v_s),
            jnp.stack(mem_k_p), jnp.stack(mem_v_p), jnp.stack(sgu_v_s))
```

```python
import functools

import jax
import jax.numpy as jnp
from jax import lax
from jax.experimental import pallas as pl
from jax.experimental.pallas import tpu as pltpu

F32 = jnp.float32
BF16 = jnp.bfloat16

D_MODEL = 1024
HEAD_DIM = 64
SB_WIDTH = 768
N_SB_HEADS = SB_WIDTH // HEAD_DIM
SGU_WIDTH = 768
MLP_CHUNK = 128
MEM_WIDTH = 256
N_MEM = 256
N_MEM_HEADS = MEM_WIDTH // HEAD_DIM
D_FF = 2816
EPS = 1e-6
SB_SCALE = HEAD_DIM ** -0.5
LOG2E = 1.4426950408889634

LANES = 128
V7X_VMEM_BYTES = 64 * 1024 * 1024
VMEM_LIMIT = V7X_VMEM_BYTES - 8 * 1024 * 1024

SB_DEAD_MASS = 106.0

SB_TQ = 256
SB_TK = 256
SB_EARLY = 160
FF_CHUNK = 256
OUT_CHUNK = 256
OUT_DEPTH = 256
PROJ_CHUNK = 256


def _cparams(*semantics):
    return pltpu.CompilerParams(dimension_semantics=semantics,
                                vmem_limit_bytes=VMEM_LIMIT)


def _const_spec(shape):
    zeros = (0,) * len(shape)
    return pl.BlockSpec(shape, lambda *_: zeros, pipeline_mode=pl.Buffered(1))


def _layer_spec(stacked, layer):
    return pl.BlockSpec((None,) + stacked.shape[1:], lambda *_: (layer, 0, 0),
                        pipeline_mode=pl.Buffered(1))


def _as_rows(g):
    return g.reshape(g.shape[0], 1, g.shape[1])


def _rmsnorm_rows(x, g):
    ms = jnp.mean(x * x, axis=-1, keepdims=True)
    return (x * lax.rsqrt(ms + EPS)) * g


def _dot_nt(a, b):
    return lax.dot_general(a, b, (((1,), (1,)), ((), ())), preferred_element_type=F32)


_PLAN_FORMS = {
    "bf16": ("bf16",),
    "f32": ("f32",),
    "f32+bf16": ("f32", "bf16"),
    "f32T": ("f32T",),
    "f32T+bf16": ("f32T", "bf16"),
    "gelu": ("f32",),
    "gelu_norm": ("f32", "bf16"),
    "gelu_norm_b": ("bf16",),
}


def _norm_proj_kernel(x_ref, g_ref, w_ref, *rest, plan, has_gate_norm):
    if has_gate_norm:
        gn_ref, out_refs = rest[0], rest[1:]
    else:
        gn_ref, out_refs = None, rest
    h = _rmsnorm_rows(x_ref[...], g_ref[...]).astype(BF16)

    def store(refs, forms, a, b, value):
        for ref, form in zip(refs, forms):
            if form == "f32T":
                ref[a:b, :] = value.T
            else:
                ref[:, a:b] = value.astype(BF16 if form == "bf16" else F32)

    k = 0
    for lo, hi, kind in plan:
        forms = _PLAN_FORMS[kind]
        refs = out_refs[k:k + len(forms)]
        k += len(forms)
        acts = []
        for a in range(lo, hi, PROJ_CHUNK):
            b = min(a + PROJ_CHUNK, hi)
            p = jnp.dot(h, w_ref[:, a:b], preferred_element_type=F32)
            if kind.startswith("gelu"):
                p = jax.nn.gelu(p)
            if kind.startswith("gelu_norm"):
                acts.append((a - lo, b - lo, p))
            else:
                store(refs, forms, a - lo, b - lo, p)
        if acts:
            ms = sum(jnp.sum(t * t, axis=-1, keepdims=True) for _, _, t in acts) / (hi - lo)
            inv = lax.rsqrt(ms + EPS)
            for a0, b0, t in acts:
                store(refs, forms, a0, b0, (t * inv) * gn_ref[:, a0:b0])


def _norm_proj(x, g, w, plan, *, tm, rows_per_batch=None, gate_norm=None, name):
    m, d = x.shape
    out_shape, out_specs = [], []
    for lo, hi, kind in plan:
        for form in _PLAN_FORMS[kind]:
            if form == "f32T":
                per = rows_per_batch // tm
                out_shape.append(jax.ShapeDtypeStruct(
                    (m // rows_per_batch, hi - lo, rows_per_batch), F32))
                out_specs.append(pl.BlockSpec(
                    (None, hi - lo, tm), lambda i, per=per: (i // per, 0, i % per)))
            else:
                out_shape.append(jax.ShapeDtypeStruct(
                    (m, hi - lo), BF16 if form == "bf16" else F32))
                out_specs.append(pl.BlockSpec((tm, hi - lo), lambda i: (i, 0)))
    layered = [g, w] + ([gate_norm] if gate_norm is not None else [])
    in_specs = [pl.BlockSpec((tm, d), lambda i: (i, 0))] + [
        _layer_spec(arr, layer) for arr, layer in layered]
    args = [x] + [arr for arr, _ in layered]
    return pl.pallas_call(
        functools.partial(_norm_proj_kernel, plan=plan,
                          has_gate_norm=gate_norm is not None),
        out_shape=out_shape, grid=(m // tm,), in_specs=in_specs,
        out_specs=out_specs, compiler_params=_cparams("arbitrary"), name=name,
    )(*args)


def _upper(n):
    j = lax.broadcasted_iota(jnp.int32, (n, n), 0)
    s = lax.broadcasted_iota(jnp.int32, (n, n), 1)
    return jnp.where(j >= s, 1.0, 0.0).astype(BF16)


def _sb_weights(z, upper, mass, causal):
    soft = jnp.maximum(z, 0.0) + jnp.log(1.0 + jnp.exp2(jnp.abs(z) * -LOG2E))
    if causal is not None:
        soft = jnp.where(causal, soft, 0.0)
    from_here = jnp.dot(soft.astype(BF16), upper, preferred_element_type=F32) + mass
    a = jnp.exp2((z - from_here) * LOG2E)
    if causal is not None:
        a = jnp.where(causal, a, 0.0)
    return a.astype(BF16), mass + jnp.sum(soft, axis=-1, keepdims=True)


def _sb_prompt_kernel(q_ref, k_ref, v_ref, y_ref, qm_ref, mk_ref, mv_ref, w_ref,
                      out_ref, mass_ref, *, tq, tk, early):
    i = pl.program_id(1)
    pairs = SB_WIDTH // LANES
    lane = lax.broadcasted_iota(jnp.int32, (1, LANES), 1)
    head0 = lane < HEAD_DIM
    upper = _upper(tk)
    mem = _mem_rows(_mem_scores(qm_ref, mk_ref), mv_ref)
    row = lax.broadcasted_iota(jnp.int32, (2 * tq, tk), 0) % tq
    col = lax.broadcasted_iota(jnp.int32, (2 * tq, tk), 1)
    has_prev = i > 0
    j_prev = jnp.maximum(i - 1, 0)

    def both_heads(x, lo, hi):
        return jnp.concatenate([x[lo:hi], x[tq + lo:tq + hi]], axis=0)

    def sweeper(p):
        lanes = slice(p * LANES, (p + 1) * LANES)
        q = q_ref[:, lanes] * jnp.asarray(SB_SCALE, BF16)
        zq = jnp.zeros_like(q)
        q2 = jnp.concatenate([jnp.where(head0, q, zq), jnp.where(head0, zq, q)], axis=0)

        def logits(qs, j):
            return _dot_nt(qs, k_ref[pl.ds(pl.multiple_of(j * tk, tk), tk), lanes])

        def values(a, j):
            n = a.shape[0] // 2
            vb = v_ref[pl.ds(pl.multiple_of(j * tk, tk), tk), lanes]
            zv = jnp.zeros_like(vb)
            return (jnp.dot(a[:n], jnp.where(head0, vb, zv), preferred_element_type=F32)
                    + jnp.dot(a[n:], jnp.where(head0, zv, vb), preferred_element_type=F32))

        return q2, logits, values

    late = tq - early
    least_early = least_late = jnp.asarray(jnp.inf, F32)
    mixed = []
    for p in range(pairs):
        q2, logits, values = sweeper(p)
        z_diag, z_prev = logits(q2, i), logits(both_heads(q2, 0, early), j_prev)
        a_diag, mass_diag = _sb_weights(z_diag, upper, jnp.zeros((2 * tq, 1), F32),
                                        col < row)
        mass_e = both_heads(mass_diag, 0, early)
        a_prev, mass_prev = _sb_weights(z_prev, upper, mass_e, None)
        acc = values(a_diag, i)
        acc_early = acc[:early] + jnp.where(has_prev, values(a_prev, j_prev), 0.0)
        mixed.append(jnp.concatenate([acc_early, acc[early:]], axis=0).astype(BF16))
        mass_e = jnp.where(has_prev, mass_prev, mass_e)
        mass_l = both_heads(mass_diag, early, tq)
        mass_ref[p] = jnp.concatenate(
            [mass_e[:early], mass_l[:late], mass_e[early:], mass_l[late:]], axis=0)
        least_early = jnp.minimum(least_early, jnp.min(mass_e))
        least_late = jnp.minimum(least_late, jnp.min(mass_l))
    out_ref[...] = _out_rows(y_ref, _mixer_part_by_depth(mixed, w_ref), mem, w_ref)

    late_row = lax.broadcasted_iota(jnp.int32, (2 * tq, 1), 0) % tq >= early

    def cond(state):
        j, least = state
        return jnp.logical_and(j >= 0, least < SB_DEAD_MASS)

    def body(state):
        j, _ = state
        fresh = jnp.logical_or(late_row, j != i - 1)
        least = jnp.asarray(jnp.inf, F32)
        for p in range(pairs):
            q2, logits, values = sweeper(p)
            mass_old = mass_ref[p]
            a, mass = _sb_weights(logits(q2, j), upper, mass_old, None)
            a = jnp.where(fresh, a, jnp.zeros_like(a))
            mass = jnp.where(fresh, mass, mass_old)
            out_ref[...] += jnp.dot(values(a, j).astype(BF16),
                                    w_ref[p * LANES:(p + 1) * LANES, :],
                                    preferred_element_type=F32)
            mass_ref[p] = mass
            least = jnp.minimum(least, jnp.min(mass))
        return j - 1, least

    j_start = jnp.where(least_late < SB_DEAD_MASS, i - 1, i - 2)
    lax.while_loop(cond, body, (j_start, jnp.minimum(least_early, least_late)))


def _sb_prompt_mix(y, q, k, v, qm, mem_kt, mem_vt, w_out, *, batch, seq, name):
    tq, tk = SB_TQ, SB_TK
    assert tq == tk and seq % tq == 0
    per = seq // tq
    q3, k3, v3 = (a.reshape(batch, seq, SB_WIDTH) for a in (q, k, v))
    q_spec = pl.BlockSpec((None, tq, SB_WIDTH), lambda b, i: (b, i, 0))
    kv_spec = pl.BlockSpec((None, seq, SB_WIDTH), lambda b, i: (b, 0, 0))
    row = lambda width: pl.BlockSpec((tq, width), lambda b, i: (b * per + i, 0))
    mem_spec = pl.BlockSpec((None, MEM_WIDTH, N_MEM), lambda b, i: (b, 0, 0))
    return pl.pallas_call(
        functools.partial(_sb_prompt_kernel, tq=tq, tk=tk, early=SB_EARLY),
        out_shape=jax.ShapeDtypeStruct((batch * seq, D_MODEL), F32),
        grid=(batch, per),
        in_specs=[q_spec, kv_spec, kv_spec, row(D_MODEL), row(MEM_WIDTH),
                  mem_spec, mem_spec, _layer_spec(*w_out)],
        out_specs=row(D_MODEL),
        scratch_shapes=[pltpu.VMEM((SB_WIDTH // LANES, 2 * tq, 1), F32)],
        compiler_params=_cparams("arbitrary", "arbitrary"),
        name=name,
    )(q3, k3, v3, y, qm, mem_kt, mem_vt, w_out[0])


def _sb_decode_kernel(q_ref, kn_ref, vn_ref, kc_hbm, vc_hbm, o_ref,
                      k_buf, v_buf, sems, acc_ref, *, t_new, tk, n_blocks):
    b = pl.program_id(0)
    n_rows = N_SB_HEADS * t_new
    last = n_blocks - 1

    def block_copies(stream, j, slot):
        keys = pl.ds(pl.multiple_of(j * tk, tk), tk)
        return (pltpu.make_async_copy(kc_hbm.at[stream, :, keys], k_buf.at[slot],
                                      sems.at[0, slot]),
                pltpu.make_async_copy(vc_hbm.at[stream, :, keys], v_buf.at[slot],
                                      sems.at[1, slot]))

    def fetch(stream, j, slot):
        for copy in block_copies(stream, j, slot):
            copy.start()

    def arrive(stream, j, slot):
        for copy in block_copies(stream, j, slot):
            copy.wait()

    @pl.when(b == 0)
    def _():
        fetch(b, last, 0)

    @pl.when(b + 1 < pl.num_programs(0))
    def _():
        fetch(b + 1, last, lax.rem(b + 1, 2))

    row_head = lax.broadcasted_iota(jnp.int32, (n_rows, SB_WIDTH), 0) // t_new
    col_head = lax.broadcasted_iota(jnp.int32, (n_rows, SB_WIDTH), 1) // HEAD_DIM
    q_rows = jnp.concatenate([q_ref[...]] * N_SB_HEADS, axis=0)
    q_bd = jnp.where(row_head == col_head, q_rows * SB_SCALE, 0.0).astype(BF16)

    n_pad = kn_ref.shape[0]
    q_idx = lax.broadcasted_iota(jnp.int32, (n_rows, n_pad), 0) % t_new
    s_idx = lax.broadcasted_iota(jnp.int32, (n_rows, n_pad), 1)
    a, mass = _sb_weights(_dot_nt(q_bd, kn_ref[...]), _upper(n_pad),
                          jnp.zeros((n_rows, 1), F32), s_idx < q_idx)
    acc_ref[...] = jnp.dot(a, vn_ref[...], preferred_element_type=F32)
    upper = _upper(tk)

    def sweep(slot, mass):
        z = jnp.dot(q_bd, k_buf[slot].astype(BF16), preferred_element_type=F32)
        a, mass = _sb_weights(z, upper, mass, None)
        acc_ref[...] += _dot_nt(a, v_buf[slot].astype(BF16))
        return mass

    own = lax.rem(b, 2)
    arrive(b, last, own)
    mass = sweep(own, mass)

    def cond(state):
        j, _, least = state
        return jnp.logical_and(j >= 0, least < SB_DEAD_MASS)

    def body(state):
        j, mass, _ = state
        fetch(b, j, 2)
        arrive(b, j, 2)
        mass = sweep(2, mass)
        return j - 1, mass, jnp.min(mass)

    lax.while_loop(cond, body, (last - 1, mass, jnp.min(mass)))

    out_head = lax.broadcasted_iota(jnp.int32, (t_new, SB_WIDTH), 1) // HEAD_DIM
    out = jnp.zeros((t_new, SB_WIDTH), F32)
    for h in range(N_SB_HEADS):
        out = out + jnp.where(out_head == h, acc_ref[h * t_new:(h + 1) * t_new, :], 0.0)
    o_ref[...] = out.astype(o_ref.dtype)


def _sb_decode(q, k_new, v_new, cache_kt, cache_vt, *, batch, t_new):
    past = cache_kt.shape[2]
    tk = SB_TK
    assert past % tk == 0
    pad = ((0, 0), (0, LANES - t_new), (0, 0))
    kn = jnp.pad(k_new.reshape(batch, t_new, SB_WIDTH), pad)
    vn = jnp.pad(v_new.reshape(batch, t_new, SB_WIDTH), pad)
    row_spec = pl.BlockSpec((None, t_new, SB_WIDTH), lambda b: (b, 0, 0))
    new_spec = pl.BlockSpec((None, LANES, SB_WIDTH), lambda b: (b, 0, 0))
    hbm_spec = pl.BlockSpec(memory_space=pl.ANY)
    out = pl.pallas_call(
        functools.partial(_sb_decode_kernel, t_new=t_new, tk=tk, n_blocks=past // tk),
        out_shape=jax.ShapeDtypeStruct((batch, t_new, SB_WIDTH), BF16),
        grid=(batch,),
        in_specs=[row_spec, new_spec, new_spec, hbm_spec, hbm_spec],
        out_specs=row_spec,
        scratch_shapes=[pltpu.VMEM((3, SB_WIDTH, tk), F32),
                        pltpu.VMEM((3, SB_WIDTH, tk), F32),
                        pltpu.SemaphoreType.DMA((2, 3)),
                        pltpu.VMEM((N_SB_HEADS * t_new, SB_WIDTH), F32)],
        compiler_params=_cparams("arbitrary"),
        name="sb_decode",
    )(q.reshape(batch, t_new, SB_WIDTH), kn, vn, cache_kt, cache_vt)
    return out.reshape(batch * t_new, SB_WIDTH)


def _heads_major(x):
    b, t, h, d = x.shape
    return jnp.transpose(x, (0, 2, 3, 1)).reshape(b, h * d, t)


def _heads_minor(xt, heads):
    b, hd, t = xt.shape
    return jnp.transpose(xt.reshape(b, heads, hd // heads, t), (0, 3, 1, 2))


def _sgu_kernel(u_ref, v_ref, w_ref, b_ref, o_ref, *, span, n_chunks):
    lane = lax.broadcasted_iota(jnp.int32, (1, LANES), 1)
    group0 = lane < HEAD_DIM
    t = lax.broadcasted_iota(jnp.int32, (span, span), 0)
    s = lax.broadcasted_iota(jnp.int32, (span, span), 1)
    tril = s <= t
    for p in range(SGU_WIDTH // LANES):
        w_pair = jnp.concatenate(
            [jnp.where(tril, w_ref[2 * p], 0.0), jnp.where(tril, w_ref[2 * p + 1], 0.0)],
            axis=1).astype(BF16)
        cols = slice(p * LANES, (p + 1) * LANES)
        for c in range(n_chunks):
            rows = slice(c * span, (c + 1) * span)
            vv = v_ref[rows, cols]
            zv = jnp.zeros_like(vv)
            v_pair = jnp.concatenate([jnp.where(group0, vv, zv),
                                      jnp.where(group0, zv, vv)], axis=0)
            mixed = jnp.dot(w_pair, v_pair, preferred_element_type=F32) + b_ref[:, cols]
            o_ref[rows, cols] = (u_ref[rows, cols] * mixed).astype(o_ref.dtype)


def _sgu(u, v, w, bias, *, tm, name):
    m = u.shape[0]
    span = w.shape[-1]
    blk = pl.BlockSpec((tm, SGU_WIDTH), lambda i: (i, 0))
    return pl.pallas_call(
        functools.partial(_sgu_kernel, span=span, n_chunks=tm // span),
        out_shape=jax.ShapeDtypeStruct((m, SGU_WIDTH), BF16),
        grid=(m // tm,),
        in_specs=[blk, blk, _const_spec(w.shape), _const_spec(bias.shape)],
        out_specs=blk, compiler_params=_cparams("arbitrary"), name=name,
    )(u, v, w, bias)


_OUT_CHUNKS = [slice(c, c + OUT_CHUNK) for c in range(0, D_MODEL, OUT_CHUNK)]


def _mem_scores(qm_ref, mk_ref):
    lane = lax.broadcasted_iota(jnp.int32, (1, MEM_WIDTH), 1) // HEAD_DIM
    qm = qm_ref[...] * jnp.asarray(SB_SCALE, BF16)
    zq = jnp.zeros_like(qm)
    mk = mk_ref[...].astype(BF16)
    return [jnp.dot(jnp.where(lane == h, qm, zq), mk, preferred_element_type=F32)
            for h in range(N_MEM_HEADS)]


def _mem_rows(scores, mv_ref):
    sublane = lax.broadcasted_iota(jnp.int32, (MEM_WIDTH, 1), 0) // HEAD_DIM
    mv = mv_ref[...].astype(BF16)
    zv = jnp.zeros_like(mv)
    mem = 0.0
    for h, s in enumerate(scores):
        e = jnp.exp(s - jnp.max(s, axis=-1, keepdims=True))
        p = e / jnp.sum(e, axis=-1, keepdims=True)
        mem = mem + _dot_nt(p.astype(BF16), jnp.where(sublane == h, mv, zv))
    return mem.astype(BF16)


def _mixer_part(o, w_ref):
    return [jnp.dot(o, w_ref[:o.shape[1], cols], preferred_element_type=F32)
            for cols in _OUT_CHUNKS]


def _mixer_part_by_depth(blocks, w_ref):
    per = OUT_DEPTH // blocks[0].shape[1]
    total = None
    for g in range(len(blocks) // per):
        o = jnp.concatenate(blocks[g * per:(g + 1) * per], axis=1)
        rows = slice(g * OUT_DEPTH, (g + 1) * OUT_DEPTH)
        share = [jnp.dot(o, w_ref[rows, cols], preferred_element_type=F32)
                 for cols in _OUT_CHUNKS]
        total = share if total is None else [s + t for s, t in zip(total, share)]
    return total


def _out_rows(y_ref, mixer_part, mem, w_ref):
    width = D_MODEL - mem.shape[1]
    return jnp.concatenate(
        [(y_ref[:, cols] + part) + jnp.dot(mem, w_ref[width:, cols],
                                           preferred_element_type=F32)
         for cols, part in zip(_OUT_CHUNKS, mixer_part)], axis=1)


def _mix_rows(y_ref, o_ref, qm_ref, mk_ref, mv_ref, w_ref):
    scores = _mem_scores(qm_ref, mk_ref)
    mixer_part = _mixer_part(o_ref[...], w_ref)
    return _out_rows(y_ref, mixer_part, _mem_rows(scores, mv_ref), w_ref)


def _gate_mix_kernel(y_ref, g_ref, w_ref, gn_ref, wsp_ref, b_ref, mk_ref, mv_ref,
                     wout_ref, out_ref, o_scr, *, span):
    h = _rmsnorm_rows(y_ref[...], g_ref[...]).astype(BF16)

    def proj(lo, hi):
        return [jnp.dot(h, w_ref[:, a:min(a + PROJ_CHUNK, hi)], preferred_element_type=F32)
                for a in range(lo, hi, PROJ_CHUNK)]

    acts = [jax.nn.gelu(p) for p in proj(SGU_WIDTH, 2 * SGU_WIDTH)]
    ms = sum(jnp.sum(t * t, axis=-1, keepdims=True) for t in acts) / SGU_WIDTH
    v = (jnp.concatenate(acts, axis=1) * lax.rsqrt(ms + EPS)) * gn_ref[...]
    u = jnp.concatenate([jax.nn.gelu(p) for p in proj(0, SGU_WIDTH)], axis=1)
    (qm,) = proj(2 * SGU_WIDTH, 2 * SGU_WIDTH + MEM_WIDTH)
    _sgu_kernel(u, v.astype(BF16), wsp_ref, b_ref, o_scr, span=span,
                n_chunks=y_ref.shape[0] // span)
    out_ref[...] = _mix_rows(y_ref, o_scr, qm.astype(BF16), mk_ref, mv_ref, wout_ref)


def _gate_mix(y, g, w_in, gate_norm, w_sp, bias, mem_kt, mem_vt, w_out, *,
              tm, rows_per_batch, name):
    m = y.shape[0]
    span = w_sp.shape[-1]
    assert rows_per_batch % tm == 0 and tm % span == 0
    per = rows_per_batch // tm
    row = pl.BlockSpec((tm, D_MODEL), lambda i: (i, 0))
    mem_spec = pl.BlockSpec((None, MEM_WIDTH, N_MEM), lambda i: (i // per, 0, 0))
    layered = [g, w_in, gate_norm]
    return pl.pallas_call(
        functools.partial(_gate_mix_kernel, span=span),
        out_shape=jax.ShapeDtypeStruct((m, D_MODEL), F32),
        grid=(m // tm,),
        in_specs=[row] + [_layer_spec(*pair) for pair in layered] + [
            _const_spec(w_sp.shape), _const_spec(bias.shape), mem_spec, mem_spec,
            _layer_spec(*w_out)],
        out_specs=row,
        scratch_shapes=[pltpu.VMEM((tm, SGU_WIDTH), BF16)],
        compiler_params=_cparams("arbitrary"), name=name,
    )(y, *[arr for arr, _ in layered], w_sp, bias, mem_kt, mem_vt, w_out[0])


def _ffn_rows(y, g_ref, wg_ref, wu_ref, wd_ref, gf_ref):
    h = _rmsnorm_rows(y, g_ref[...]).astype(BF16)
    acc = y
    for c in range(D_FF // FF_CHUNK):
        cols = slice(c * FF_CHUNK, (c + 1) * FF_CHUNK)
        gate = jnp.dot(h, wg_ref[:, cols], preferred_element_type=F32)
        up = jnp.dot(h, wu_ref[:, cols], preferred_element_type=F32)
        act = (jax.nn.silu(gate) * up).astype(BF16)
        acc = acc + jnp.dot(act, wd_ref[cols, :], preferred_element_type=F32)
    return acc if gf_ref is None else _rmsnorm_rows(acc, gf_ref[...])


def _mix_ffn_kernel(*refs, mix, ffn, final):
    refs = list(refs)
    out_ref = refs.pop()
    if mix:
        y = _mix_rows(*refs[:6])
        del refs[:6]
    else:
        y = refs.pop(0)[...]
    if ffn:
        y = _ffn_rows(y, *refs[:4], refs[4] if final else None)
    out_ref[...] = y


def _mix_ffn(y, *, mix=None, ffn=None, g_final=None, tm, name):
    m = y.shape[0]
    row = pl.BlockSpec((tm, D_MODEL), lambda i: (i, 0))
    args, in_specs = [y], [row]
    if mix is not None:
        o, qm, mem_kt, mem_vt, w_out, layer, rows_per_batch = mix
        assert rows_per_batch % tm == 0
        per = rows_per_batch // tm
        mem_spec = pl.BlockSpec((None, MEM_WIDTH, N_MEM), lambda i: (i // per, 0, 0))
        args += [o, qm, mem_kt, mem_vt, w_out]
        in_specs += [pl.BlockSpec((tm, SB_WIDTH), lambda i: (i, 0)),
                     pl.BlockSpec((tm, MEM_WIDTH), lambda i: (i, 0)),
                     mem_spec, mem_spec, _layer_spec(w_out, layer)]
    if ffn is not None:
        *stacked, layer = ffn
        args += stacked
        in_specs += [_layer_spec(a, layer) for a in stacked]
        if g_final is not None:
            args.append(g_final.reshape(1, D_MODEL))
            in_specs.append(_const_spec((1, D_MODEL)))
    return pl.pallas_call(
        functools.partial(_mix_ffn_kernel, mix=mix is not None, ffn=ffn is not None,
                          final=g_final is not None),
        out_shape=jax.ShapeDtypeStruct((m, D_MODEL), F32),
        grid=(m // tm,), in_specs=in_specs, out_specs=row,
        compiler_params=_cparams("arbitrary"), name=name,
    )(*args)


_PLAN_A_PROMPT = ((0, SB_WIDTH, "bf16"), (SB_WIDTH, 2 * SB_WIDTH, "f32T+bf16"),
                  (2 * SB_WIDTH, 3 * SB_WIDTH, "f32T+bf16"),
                  (3 * SB_WIDTH, 3 * SB_WIDTH + MEM_WIDTH, "bf16"))
_PLAN_A_SAMPLE = ((0, SB_WIDTH, "f32"), (SB_WIDTH, 2 * SB_WIDTH, "f32+bf16"),
                  (2 * SB_WIDTH, 3 * SB_WIDTH, "f32+bf16"),
                  (3 * SB_WIDTH, 3 * SB_WIDTH + MEM_WIDTH, "bf16"))
_PLAN_B_PROMPT = ((SGU_WIDTH, 2 * SGU_WIDTH, "gelu_norm_b"), (0, SGU_WIDTH, "gelu"),
                  (2 * SGU_WIDTH, 2 * SGU_WIDTH + MEM_WIDTH, "bf16"))
_PLAN_B_SAMPLE = ((SGU_WIDTH, 2 * SGU_WIDTH, "gelu_norm"),) + _PLAN_B_PROMPT[1:]
_PLAN_MEM = ((0, MEM_WIDTH, "f32T"), (MEM_WIDTH, 2 * MEM_WIDTH, "f32T"))


def kernel(x_prompt, x_sample, cache_sb_k, cache_sb_v, cache_mem_k, cache_mem_v,
           mem_prompt, g_mix, w_in_a, w_in_b, w_sp, b_sp, g_sgu, g_mem, w_mem_kv,
           w_out, g_ffn, w_gate, w_up, w_down, g_final):
    batch, seq, _ = x_prompt.shape
    dec_batch, dec_seq, _ = x_sample.shape
    depth = g_mix.shape[0]
    tm_p = 1024
    tm_gate = 1024
    tm_s = dec_batch * dec_seq

    y_p = x_prompt.reshape(batch * seq, D_MODEL)
    y_s = x_sample.reshape(tm_s, D_MODEL)
    mem_rows = mem_prompt.reshape(batch * N_MEM, D_MODEL)

    g_mix, g_mem, g_ffn, g_sgu = (_as_rows(g) for g in (g_mix, g_mem, g_ffn, g_sgu))
    w_in_a, w_in_b, w_mem_kv, w_out, w_gate, w_up, w_down = (
        w.astype(BF16) for w in (w_in_a, w_in_b, w_mem_kv, w_out, w_gate, w_up, w_down))

    sb_k_p, sb_v_p, sb_k_s, sb_v_s, mem_k_p, mem_v_p, sgu_v_s = ([] for _ in range(7))
    for l in range(depth):
        mkt_p, mvt_p = _norm_proj(mem_rows, (g_mem, l), (w_mem_kv, l), _PLAN_MEM,
                                  tm=N_MEM, rows_per_batch=N_MEM, name=f"mem_kv{l}")
        mem_k_p.append(_heads_minor(mkt_p, N_MEM_HEADS))
        mem_v_p.append(_heads_minor(mvt_p, N_MEM_HEADS))
        mkt_s, mvt_s = _heads_major(cache_mem_k[l]), _heads_major(cache_mem_v[l])

        if l % 2 == 0:
            ia = l // 2
            q, kt_f, k_b, vt_f, v_b, qm_p = _norm_proj(
                y_p, (g_mix, l), (w_in_a, ia), _PLAN_A_PROMPT, tm=tm_p,
                rows_per_batch=seq, name=f"in_a{l}_p")
            sb_k_p.append(_heads_minor(kt_f, N_SB_HEADS))
            sb_v_p.append(_heads_minor(vt_f, N_SB_HEADS))
            y_p = _sb_prompt_mix(y_p, q, k_b, v_b, qm_p, mkt_p, mvt_p, (w_out, l),
                                 batch=batch, seq=seq, name=f"sb_mix{l}_p")

            q, k_f, k_b, v_f, v_b, qm_s = _norm_proj(
                y_s, (g_mix, l), (w_in_a, ia), _PLAN_A_SAMPLE, tm=tm_s, name=f"in_a{l}_s")
            sb_k_s.append(k_f.reshape(dec_batch, dec_seq, N_SB_HEADS, HEAD_DIM))
            sb_v_s.append(v_f.reshape(dec_batch, dec_seq, N_SB_HEADS, HEAD_DIM))
            o_s = _sb_decode(q, k_b, v_b, _heads_major(cache_sb_k[ia]),
                             _heads_major(cache_sb_v[ia]),
                             batch=dec_batch, t_new=dec_seq)
        else:
            ib = l // 2
            span = min(seq, MLP_CHUNK)
            bias = jnp.repeat(b_sp[ib][:, :span].T, HEAD_DIM, axis=1)
            y_p = _gate_mix(y_p, (g_mix, l), (w_in_b, ib), (g_sgu, ib),
                            w_sp[ib][:, :span, :span], bias, mkt_p, mvt_p, (w_out, l),
                            tm=tm_gate, rows_per_batch=seq, name=f"gate_mix{l}_p")

            v_f, v_b, u, qm_s = _norm_proj(
                y_s, (g_mix, l), (w_in_b, ib), _PLAN_B_SAMPLE, tm=tm_s,
                gate_norm=(g_sgu, ib), name=f"in_b{l}_s")
            sgu_v_s.append(v_f.reshape(dec_batch, dec_seq, SGU_WIDTH))
            span = min(dec_seq, MLP_CHUNK)
            reps = MLP_CHUNK // span
            eye = jnp.eye(reps, dtype=F32)
            w_bd = jnp.einsum("ab,gts->gatbs", eye, w_sp[ib][:, :span, :span]).reshape(
                -1, MLP_CHUNK, MLP_CHUNK)
            bias = jnp.tile(jnp.repeat(b_sp[ib][:, :span].T, HEAD_DIM, axis=1), (reps, 1))
            o_s = _sgu(u, v_b, w_bd, bias, tm=tm_s, name=f"sgu{l}_s")

        last = g_final if l == depth - 1 else None
        ffn = (g_ffn, w_gate, w_up, w_down, l)
        y_p = _mix_ffn(y_p, ffn=ffn, g_final=last, tm=tm_p, name=f"ffn{l}_p")
        y_s = _mix_ffn(y_s, mix=(o_s, qm_s, mkt_s, mvt_s, w_out, l, dec_seq),
                       tm=dec_seq, name=f"mix_out{l}_s")
        y_s = _mix_ffn(y_s, ffn=ffn, g_final=last, tm=tm_s, name=f"ffn{l}_s")

    return (y_p.reshape(batch, seq, D_MODEL), y_s.reshape(dec_batch, dec_seq, D_MODEL),
            jnp.stack(sb_k_p), jnp.stack(sb_v_p), jnp.stack(sb_k_s), jnp.stack(sb_v_s),
            jnp.stack(mem_k_p), jnp.stack(mem_v_p), jnp.stack(sgu_v_s))
```

```python
import functools

import jax
import jax.numpy as jnp
from jax import lax
from jax.experimental import pallas as pl
from jax.experimental.pallas import tpu as pltpu

F32 = jnp.float32
BF16 = jnp.bfloat16

D_MODEL = 1024
HEAD_DIM = 64
SB_WIDTH = 768
N_SB_HEADS = SB_WIDTH // HEAD_DIM
SGU_WIDTH = 768
MLP_CHUNK = 128
MEM_WIDTH = 256
N_MEM = 256
N_MEM_HEADS = MEM_WIDTH // HEAD_DIM
D_FF = 2816
EPS = 1e-6
SB_SCALE = HEAD_DIM ** -0.5
LOG2E = 1.4426950408889634

LANES = 128
V7X_VMEM_BYTES = 64 * 1024 * 1024
VMEM_LIMIT = V7X_VMEM_BYTES - 8 * 1024 * 1024

SB_DEAD_MASS = 106.0

SB_TQ = 256
SB_TK = 256
SB_EARLY = 160
FF_CHUNK = 256
OUT_CHUNK = 256
OUT_DEPTH = 256
PROJ_CHUNK = 256


def _cparams(*semantics):
    return pltpu.CompilerParams(dimension_semantics=semantics,
                                vmem_limit_bytes=VMEM_LIMIT)


def _const_spec(shape):
    zeros = (0,) * len(shape)
    return pl.BlockSpec(shape, lambda *_: zeros, pipeline_mode=pl.Buffered(1))


def _layer_spec(stacked, layer):
    return pl.BlockSpec((None,) + stacked.shape[1:], lambda *_: (layer, 0, 0),
                        pipeline_mode=pl.Buffered(1))


def _as_rows(g):
    return g.reshape(g.shape[0], 1, g.shape[1])


def _rmsnorm_rows(x, g):
    ms = jnp.mean(x * x, axis=-1, keepdims=True)
    return (x * lax.rsqrt(ms + EPS)) * g


def _dot_nt(a, b):
    return lax.dot_general(a, b, (((1,), (1,)), ((), ())), preferred_element_type=F32)


_PLAN_FORMS = {
    "bf16": ("bf16",),
    "f32": ("f32",),
    "f32+bf16": ("f32", "bf16"),
    "f32T": ("f32T",),
    "f32T+bf16": ("f32T", "bf16"),
    "gelu": ("f32",),
    "gelu_norm": ("f32", "bf16"),
    "gelu_norm_b": ("bf16",),
}


def _norm_proj_kernel(x_ref, g_ref, w_ref, *rest, plan, has_gate_norm):
    if has_gate_norm:
        gn_ref, out_refs = rest[0], rest[1:]
    else:
        gn_ref, out_refs = None, rest
    h = _rmsnorm_rows(x_ref[...], g_ref[...]).astype(BF16)

    def store(refs, forms, a, b, value):
        for ref, form in zip(refs, forms):
            if form == "f32T":
                ref[a:b, :] = value.T
            else:
                ref[:, a:b] = value.astype(BF16 if form == "bf16" else F32)

    k = 0
    for lo, hi, kind in plan:
        forms = _PLAN_FORMS[kind]
        refs = out_refs[k:k + len(forms)]
        k += len(forms)
        acts = []
        for a in range(lo, hi, PROJ_CHUNK):
            b = min(a + PROJ_CHUNK, hi)
            p = jnp.dot(h, w_ref[:, a:b], preferred_element_type=F32)
            if kind.startswith("gelu"):
                p = jax.nn.gelu(p)
            if kind.startswith("gelu_norm"):
                acts.append((a - lo, b - lo, p))
            else:
                store(refs, forms, a - lo, b - lo, p)
        if acts:
            ms = sum(jnp.sum(t * t, axis=-1, keepdims=True) for _, _, t in acts) / (hi - lo)
            inv = lax.rsqrt(ms + EPS)
            for a0, b0, t in acts:
                store(refs, forms, a0, b0, (t * inv) * gn_ref[:, a0:b0])


def _norm_proj(x, g, w, plan, *, tm, rows_per_batch=None, gate_norm=None, name):
    m, d = x.shape
    out_shape, out_specs = [], []
    for lo, hi, kind in plan:
        for form in _PLAN_FORMS[kind]:
            if form == "f32T":
                per = rows_per_batch // tm
                out_shape.append(jax.ShapeDtypeStruct(
                    (m // rows_per_batch, hi - lo, rows_per_batch), F32))
                out_specs.append(pl.BlockSpec(
                    (None, hi - lo, tm), lambda i, per=per: (i // per, 0, i % per)))
            else:
                out_shape.append(jax.ShapeDtypeStruct(
                    (m, hi - lo), BF16 if form == "bf16" else F32))
                out_specs.append(pl.BlockSpec((tm, hi - lo), lambda i: (i, 0)))
    layered = [g, w] + ([gate_norm] if gate_norm is not None else [])
    in_specs = [pl.BlockSpec((tm, d), lambda i: (i, 0))] + [
        _layer_spec(arr, layer) for arr, layer in layered]
    args = [x] + [arr for arr, _ in layered]
    return pl.pallas_call(
        functools.partial(_norm_proj_kernel, plan=plan,
                          has_gate_norm=gate_norm is not None),
        out_shape=out_shape, grid=(m // tm,), in_specs=in_specs,
        out_specs=out_specs, compiler_params=_cparams("arbitrary"), name=name,
    )(*args)


def _upper(n):
    j = lax.broadcasted_iota(jnp.int32, (n, n), 0)
    s = lax.broadcasted_iota(jnp.int32, (n, n), 1)
    return jnp.where(j >= s, 1.0, 0.0).astype(BF16)


def _sb_weights(z, upper, mass, causal):
    soft = jnp.maximum(z, 0.0) + jnp.log(1.0 + jnp.exp2(jnp.abs(z) * -LOG2E))
    if causal is not None:
        soft = jnp.where(causal, soft, 0.0)
    from_here = jnp.dot(soft.astype(BF16), upper, preferred_element_type=F32) + mass
    a = jnp.exp2((z - from_here) * LOG2E)
    if causal is not None:
        a = jnp.where(causal, a, 0.0)
    return a.astype(BF16), mass + jnp.sum(soft, axis=-1, keepdims=True)


def _sb_prompt_kernel(q_ref, k_ref, v_ref, y_ref, qm_ref, mk_ref, mv_ref, w_ref,
                      out_ref, mass_ref, *, tq, tk, early):
    i = pl.program_id(1)
    pairs = SB_WIDTH // LANES
    lane = lax.broadcasted_iota(jnp.int32, (1, LANES), 1)
    head0 = lane < HEAD_DIM
    upper = _upper(tk)
    mem = _mem_rows(_mem_scores(qm_ref, mk_ref), mv_ref)
    row = lax.broadcasted_iota(jnp.int32, (2 * tq, tk), 0) % tq
    col = lax.broadcasted_iota(jnp.int32, (2 * tq, tk), 1)
    has_prev = i > 0
    j_prev = jnp.maximum(i - 1, 0)

    def both_heads(x, lo, hi):
        return jnp.concatenate([x[lo:hi], x[tq + lo:tq + hi]], axis=0)

    def sweeper(p):
        lanes = slice(p * LANES, (p + 1) * LANES)
        q = q_ref[:, lanes] * jnp.asarray(SB_SCALE, BF16)
        zq = jnp.zeros_like(q)
        q2 = jnp.concatenate([jnp.where(head0, q, zq), jnp.where(head0, zq, q)], axis=0)

        def logits(qs, j):
            return _dot_nt(qs, k_ref[pl.ds(pl.multiple_of(j * tk, tk), tk), lanes])

        def values(a, j):
            n = a.shape[0] // 2
            vb = v_ref[pl.ds(pl.multiple_of(j * tk, tk), tk), lanes]
            zv = jnp.zeros_like(vb)
            return (jnp.dot(a[:n], jnp.where(head0, vb, zv), preferred_element_type=F32)
                    + jnp.dot(a[n:], jnp.where(head0, zv, vb), preferred_element_type=F32))

        return q2, logits, values

    late = tq - early
    least_early = least_late = jnp.asarray(jnp.inf, F32)
    mixed = []
    for p in range(pairs):
        q2, logits, values = sweeper(p)
        z_diag, z_prev = logits(q2, i), logits(both_heads(q2, 0, early), j_prev)
        a_diag, mass_diag = _sb_weights(z_diag, upper, jnp.zeros((2 * tq, 1), F32),
                                        col < row)
        mass_e = both_heads(mass_diag, 0, early)
        a_prev, mass_prev = _sb_weights(z_prev, upper, mass_e, None)
        acc = values(a_diag, i)
        acc_early = acc[:early] + jnp.where(has_prev, values(a_prev, j_prev), 0.0)
        mixed.append(jnp.concatenate([acc_early, acc[early:]], axis=0).astype(BF16))
        mass_e = jnp.where(has_prev, mass_prev, mass_e)
        mass_l = both_heads(mass_diag, early, tq)
        mass_ref[p] = jnp.concatenate(
            [mass_e[:early], mass_l[:late], mass_e[early:], mass_l[late:]], axis=0)
        least_early = jnp.minimum(least_early, jnp.min(mass_e))
        least_late = jnp.minimum(least_late, jnp.min(mass_l))
    out_ref[...] = _out_rows(y_ref, _mixer_part_by_depth(mixed, w_ref), mem, w_ref)

    late_row = lax.broadcasted_iota(jnp.int32, (2 * tq, 1), 0) % tq >= early

    def cond(state):
        j, least = state
        return jnp.logical_and(j >= 0, least < SB_DEAD_MASS)

    def body(state):
        j, _ = state
        fresh = jnp.logical_or(late_row, j != i - 1)
        least = jnp.asarray(jnp.inf, F32)
        for p in range(pairs):
            q2, logits, values = sweeper(p)
            mass_old = mass_ref[p]
            a, mass = _sb_weights(logits(q2, j), upper, mass_old, None)
            a = jnp.where(fresh, a, jnp.zeros_like(a))
            mass = jnp.where(fresh, mass, mass_old)
            out_ref[...] += jnp.dot(values(a, j).astype(BF16),
                                    w_ref[p * LANES:(p + 1) * LANES, :],
                                    preferred_element_type=F32)
            mass_ref[p] = mass
            least = jnp.minimum(least, jnp.min(mass))
        return j - 1, least

    j_start = jnp.where(least_late < SB_DEAD_MASS, i - 1, i - 2)
    lax.while_loop(cond, body, (j_start, jnp.minimum(least_early, least_late)))


def _sb_prompt_mix(y, q, k, v, qm, mem_kt, mem_vt, w_out, *, batch, seq, name):
    tq, tk = SB_TQ, SB_TK
    assert tq == tk and seq % tq == 0
    per = seq // tq
    q3, k3, v3 = (a.reshape(batch, seq, SB_WIDTH) for a in (q, k, v))
    q_spec = pl.BlockSpec((None, tq, SB_WIDTH), lambda b, i: (b, i, 0))
    kv_spec = pl.BlockSpec((None, seq, SB_WIDTH), lambda b, i: (b, 0, 0))
    row = lambda width: pl.BlockSpec((tq, width), lambda b, i: (b * per + i, 0))
    mem_spec = pl.BlockSpec((None, MEM_WIDTH, N_MEM), lambda b, i: (b, 0, 0))
    return pl.pallas_call(
        functools.partial(_sb_prompt_kernel, tq=tq, tk=tk, early=SB_EARLY),
        out_shape=jax.ShapeDtypeStruct((batch * seq, D_MODEL), F32),
        grid=(batch, per),
        in_specs=[q_spec, kv_spec, kv_spec, row(D_MODEL), row(MEM_WIDTH),
                  mem_spec, mem_spec, _layer_spec(*w_out)],
        out_specs=row(D_MODEL),
        scratch_shapes=[pltpu.VMEM((SB_WIDTH // LANES, 2 * tq, 1), F32)],
        compiler_params=_cparams("arbitrary", "arbitrary"),
        name=name,
    )(q3, k3, v3, y, qm, mem_kt, mem_vt, w_out[0])


def _sb_decode_kernel(q_ref, kn_ref, vn_ref, kc_hbm, vc_hbm, o_ref,
                      k_buf, v_buf, sems, acc_ref, *, t_new, tk, n_blocks):
    b = pl.program_id(0)
    n_rows = N_SB_HEADS * t_new
    last = n_blocks - 1

    def block_copies(stream, j, slot):
        keys = pl.ds(pl.multiple_of(j * tk, tk), tk)
        return (pltpu.make_async_copy(kc_hbm.at[stream, :, keys], k_buf.at[slot],
                                      sems.at[0, slot]),
                pltpu.make_async_copy(vc_hbm.at[stream, :, keys], v_buf.at[slot],
                                      sems.at[1, slot]))

    def fetch(stream, j, slot):
        for copy in block_copies(stream, j, slot):
            copy.start()

    def arrive(stream, j, slot):
        for copy in block_copies(stream, j, slot):
            copy.wait()

    @pl.when(b == 0)
    def _():
        fetch(b, last, 0)

    @pl.when(b + 1 < pl.num_programs(0))
    def _():
        fetch(b + 1, last, lax.rem(b + 1, 2))

    row_head = lax.broadcasted_iota(jnp.int32, (n_rows, SB_WIDTH), 0) // t_new
    col_head = lax.broadcasted_iota(jnp.int32, (n_rows, SB_WIDTH), 1) // HEAD_DIM
    q_rows = jnp.concatenate([q_ref[...]] * N_SB_HEADS, axis=0)
    q_bd = jnp.where(row_head == col_head, q_rows * SB_SCALE, 0.0).astype(BF16)

    n_pad = kn_ref.shape[0]
    q_idx = lax.broadcasted_iota(jnp.int32, (n_rows, n_pad), 0) % t_new
    s_idx = lax.broadcasted_iota(jnp.int32, (n_rows, n_pad), 1)
    a, mass = _sb_weights(_dot_nt(q_bd, kn_ref[...]), _upper(n_pad),
                          jnp.zeros((n_rows, 1), F32), s_idx < q_idx)
    acc_ref[...] = jnp.dot(a, vn_ref[...], preferred_element_type=F32)
    upper = _upper(tk)

    def sweep(slot, mass):
        z = jnp.dot(q_bd, k_buf[slot].astype(BF16), preferred_element_type=F32)
        a, mass = _sb_weights(z, upper, mass, None)
        acc_ref[...] += _dot_nt(a, v_buf[slot].astype(BF16))
        return mass

    own = lax.rem(b, 2)
    arrive(b, last, own)
    mass = sweep(own, mass)

    def cond(state):
        j, _, least = state
        return jnp.logical_and(j >= 0, least < SB_DEAD_MASS)

    def body(state):
        j, mass, _ = state
        fetch(b, j, 2)
        arrive(b, j, 2)
        mass = sweep(2, mass)
        return j - 1, mass, jnp.min(mass)

    lax.while_loop(cond, body, (last - 1, mass, jnp.min(mass)))

    out_head = lax.broadcasted_iota(jnp.int32, (t_new, SB_WIDTH), 1) // HEAD_DIM
    out = jnp.zeros((t_new, SB_WIDTH), F32)
    for h in range(N_SB_HEADS):
        out = out + jnp.where(out_head == h, acc_ref[h * t_new:(h + 1) * t_new, :], 0.0)
    o_ref[...] = out.astype(o_ref.dtype)


def _sb_decode(q, k_new, v_new, cache_kt, cache_vt, *, batch, t_new):
    past = cache_kt.shape[2]
    tk = SB_TK
    assert past % tk == 0
    pad = ((0, 0), (0, LANES - t_new), (0, 0))
    kn = jnp.pad(k_new.reshape(batch, t_new, SB_WIDTH), pad)
    vn = jnp.pad(v_new.reshape(batch, t_new, SB_WIDTH), pad)
    row_spec = pl.BlockSpec((None, t_new, SB_WIDTH), lambda b: (b, 0, 0))
    new_spec = pl.BlockSpec((None, LANES, SB_WIDTH), lambda b: (b, 0, 0))
    hbm_spec = pl.BlockSpec(memory_space=pl.ANY)
    out = pl.pallas_call(
        functools.partial(_sb_decode_kernel, t_new=t_new, tk=tk, n_blocks=past // tk),
        out_shape=jax.ShapeDtypeStruct((batch, t_new, SB_WIDTH), BF16),
        grid=(batch,),
        in_specs=[row_spec, new_spec, new_spec, hbm_spec, hbm_spec],
        out_specs=row_spec,
        scratch_shapes=[pltpu.VMEM((3, SB_WIDTH, tk), F32),
                        pltpu.VMEM((3, SB_WIDTH, tk), F32),
                        pltpu.SemaphoreType.DMA((2, 3)),
                        pltpu.VMEM((N_SB_HEADS * t_new, SB_WIDTH), F32)],
        compiler_params=_cparams("arbitrary"),
        name="sb_decode",
    )(q.reshape(batch, t_new, SB_WIDTH), kn, vn, cache_kt, cache_vt)
    return out.reshape(batch * t_new, SB_WIDTH)


def _heads_major(x):
    b, t, h, d = x.shape
    return jnp.transpose(x, (0, 2, 3, 1)).reshape(b, h * d, t)


def _heads_minor(xt, heads):
    b, hd, t = xt.shape
    return jnp.transpose(xt.reshape(b, heads, hd // heads, t), (0, 3, 1, 2))


def _sgu_kernel(u_ref, v_ref, w_ref, b_ref, o_ref, *, span, n_chunks):
    lane = lax.broadcasted_iota(jnp.int32, (1, LANES), 1)
    group0 = lane < HEAD_DIM
    t = lax.broadcasted_iota(jnp.int32, (span, span), 0)
    s = lax.broadcasted_iota(jnp.int32, (span, span), 1)
    tril = s <= t
    for p in range(SGU_WIDTH // LANES):
        w_pair = jnp.concatenate(
            [jnp.where(tril, w_ref[2 * p], 0.0), jnp.where(tril, w_ref[2 * p + 1], 0.0)],
            axis=1).astype(BF16)
        cols = slice(p * LANES, (p + 1) * LANES)
        for c in range(n_chunks):
            rows = slice(c * span, (c + 1) * span)
            vv = v_ref[rows, cols]
            zv = jnp.zeros_like(vv)
            v_pair = jnp.concatenate([jnp.where(group0, vv, zv),
                                      jnp.where(group0, zv, vv)], axis=0)
            mixed = jnp.dot(w_pair, v_pair, preferred_element_type=F32) + b_ref[:, cols]
            o_ref[rows, cols] = (u_ref[rows, cols] * mixed).astype(o_ref.dtype)


def _sgu(u, v, w, bias, *, tm, name):
    m = u.shape[0]
    span = w.shape[-1]
    blk = pl.BlockSpec((tm, SGU_WIDTH), lambda i: (i, 0))
    return pl.pallas_call(
        functools.partial(_sgu_kernel, span=span, n_chunks=tm // span),
        out_shape=jax.ShapeDtypeStruct((m, SGU_WIDTH), BF16),
        grid=(m // tm,),
        in_specs=[blk, blk, _const_spec(w.shape), _const_spec(bias.shape)],
        out_specs=blk, compiler_params=_cparams("arbitrary"), name=name,
    )(u, v, w, bias)


_OUT_CHUNKS = [slice(c, c + OUT_CHUNK) for c in range(0, D_MODEL, OUT_CHUNK)]


def _mem_scores(qm_ref, mk_ref):
    lane = lax.broadcasted_iota(jnp.int32, (1, MEM_WIDTH), 1) // HEAD_DIM
    qm = qm_ref[...] * jnp.asarray(SB_SCALE, BF16)
    zq = jnp.zeros_like(qm)
    mk = mk_ref[...].astype(BF16)
    return [jnp.dot(jnp.where(lane == h, qm, zq), mk, preferred_element_type=F32)
            for h in range(N_MEM_HEADS)]


def _mem_rows(scores, mv_ref):
    sublane = lax.broadcasted_iota(jnp.int32, (MEM_WIDTH, 1), 0) // HEAD_DIM
    mv = mv_ref[...].astype(BF16)
    zv = jnp.zeros_like(mv)
    mem = 0.0
    for h, s in enumerate(scores):
        e = jnp.exp(s - jnp.max(s, axis=-1, keepdims=True))
        p = e / jnp.sum(e, axis=-1, keepdims=True)
        mem = mem + _dot_nt(p.astype(BF16), jnp.where(sublane == h, mv, zv))
    return mem.astype(BF16)


def _mixer_part(o, w_ref):
    return [jnp.dot(o, w_ref[:o.shape[1], cols], preferred_element_type=F32)
            for cols in _OUT_CHUNKS]


def _mixer_part_by_depth(blocks, w_ref):
    per = OUT_DEPTH // blocks[0].shape[1]
    total = None
    for g in range(len(blocks) // per):
        o = jnp.concatenate(blocks[g * per:(g + 1) * per], axis=1)
        rows = slice(g * OUT_DEPTH, (g + 1) * OUT_DEPTH)
        share = [jnp.dot(o, w_ref[rows, cols], preferred_element_type=F32)
                 for cols in _OUT_CHUNKS]
        total = share if total is None else [s + t for s, t in zip(total, share)]
    return total


def _out_rows(y_ref, mixer_part, mem, w_ref):
    width = D_MODEL - mem.shape[1]
    return jnp.concatenate(
        [(y_ref[:, cols] + part) + jnp.dot(mem, w_ref[width:, cols],
                                           preferred_element_type=F32)
         for cols, part in zip(_OUT_CHUNKS, mixer_part)], axis=1)


def _mix_rows(y_ref, o_ref, qm_ref, mk_ref, mv_ref, w_ref):
    scores = _mem_scores(qm_ref, mk_ref)
    mixer_part = _mixer_part(o_ref[...], w_ref)
    return _out_rows(y_ref, mixer_part, _mem_rows(scores, mv_ref), w_ref)


def _gate_mix_kernel(y_ref, g_ref, w_ref, gn_ref, wsp_ref, b_ref, mk_ref, mv_ref,
                     wout_ref, out_ref, o_scr, *, span):
    h = _rmsnorm_rows(y_ref[...], g_ref[...]).astype(BF16)

    def proj(lo, hi):
        return [jnp.dot(h, w_ref[:, a:min(a + PROJ_CHUNK, hi)], preferred_element_type=F32)
                for a in range(lo, hi, PROJ_CHUNK)]

    acts = [jax.nn.gelu(p) for p in proj(SGU_WIDTH, 2 * SGU_WIDTH)]
    ms = sum(jnp.sum(t * t, axis=-1, keepdims=True) for t in acts) / SGU_WIDTH
    v = (jnp.concatenate(acts, axis=1) * lax.rsqrt(ms + EPS)) * gn_ref[...]
    u = jnp.concatenate([jax.nn.gelu(p) for p in proj(0, SGU_WIDTH)], axis=1)
    (qm,) = proj(2 * SGU_WIDTH, 2 * SGU_WIDTH + MEM_WIDTH)
    _sgu_kernel(u, v.astype(BF16), wsp_ref, b_ref, o_scr, span=span,
                n_chunks=y_ref.shape[0] // span)
    out_ref[...] = _mix_rows(y_ref, o_scr, qm.astype(BF16), mk_ref, mv_ref, wout_ref)


def _gate_mix(y, g, w_in, gate_norm, w_sp, bias, mem_kt, mem_vt, w_out, *,
              tm, rows_per_batch, name):
    m = y.shape[0]
    span = w_sp.shape[-1]
    assert rows_per_batch % tm == 0 and tm % span == 0
    per = rows_per_batch // tm
    row = pl.BlockSpec((tm, D_MODEL), lambda i: (i, 0))
    mem_spec = pl.BlockSpec((None, MEM_WIDTH, N_MEM), lambda i: (i // per, 0, 0))
    layered = [g, w_in, gate_norm]
    return pl.pallas_call(
        functools.partial(_gate_mix_kernel, span=span),
        out_shape=jax.ShapeDtypeStruct((m, D_MODEL), F32),
        grid=(m // tm,),
        in_specs=[row] + [_layer_spec(*pair) for pair in layered] + [
            _const_spec(w_sp.shape), _const_spec(bias.shape), mem_spec, mem_spec,
            _layer_spec(*w_out)],
        out_specs=row,
        scratch_shapes=[pltpu.VMEM((tm, SGU_WIDTH), BF16)],
        compiler_params=_cparams("arbitrary"), name=name,
    )(y, *[arr for arr, _ in layered], w_sp, bias, mem_kt, mem_vt, w_out[0])


def _ffn_rows(y, g_ref, wg_ref, wu_ref, wd_ref, gf_ref):
    h = _rmsnorm_rows(y, g_ref[...]).astype(BF16)
    acts = []
    for c in range(D_FF // FF_CHUNK):
        cols = slice(c * FF_CHUNK, (c + 1) * FF_CHUNK)
        gate = jnp.dot(h, wg_ref[:, cols], preferred_element_type=F32)
        up = jnp.dot(h, wu_ref[:, cols], preferred_element_type=F32)
        acts.append((jax.nn.silu(gate) * up).astype(BF16))
    acc = y + jnp.dot(jnp.concatenate(acts, axis=1), wd_ref[...],
                      preferred_element_type=F32)
    return acc if gf_ref is None else _rmsnorm_rows(acc, gf_ref[...])


def _mix_ffn_kernel(*refs, mix, ffn, final):
    refs = list(refs)
    out_ref = refs.pop()
    if mix:
        y = _mix_rows(*refs[:6])
        del refs[:6]
    else:
        y = refs.pop(0)[...]
    if ffn:
        y = _ffn_rows(y, *refs[:4], refs[4] if final else None)
    out_ref[...] = y


def _mix_ffn(y, *, mix=None, ffn=None, g_final=None, tm, name):
    m = y.shape[0]
    row = pl.BlockSpec((tm, D_MODEL), lambda i: (i, 0))
    args, in_specs = [y], [row]
    if mix is not None:
        o, qm, mem_kt, mem_vt, w_out, layer, rows_per_batch = mix
        assert rows_per_batch % tm == 0
        per = rows_per_batch // tm
        mem_spec = pl.BlockSpec((None, MEM_WIDTH, N_MEM), lambda i: (i // per, 0, 0))
        args += [o, qm, mem_kt, mem_vt, w_out]
        in_specs += [pl.BlockSpec((tm, SB_WIDTH), lambda i: (i, 0)),
                     pl.BlockSpec((tm, MEM_WIDTH), lambda i: (i, 0)),
                     mem_spec, mem_spec, _layer_spec(w_out, layer)]
    if ffn is not None:
        *stacked, layer = ffn
        args += stacked
        in_specs += [_layer_spec(a, layer) for a in stacked]
        if g_final is not None:
            args.append(g_final.reshape(1, D_MODEL))
            in_specs.append(_const_spec((1, D_MODEL)))
    return pl.pallas_call(
        functools.partial(_mix_ffn_kernel, mix=mix is not None, ffn=ffn is not None,
                          final=g_final is not None),
        out_shape=jax.ShapeDtypeStruct((m, D_MODEL), F32),
        grid=(m // tm,), in_specs=in_specs, out_specs=row,
        compiler_params=_cparams("arbitrary"), name=name,
    )(*args)


_PLAN_A_PROMPT = ((0, SB_WIDTH, "bf16"), (SB_WIDTH, 2 * SB_WIDTH, "f32T+bf16"),
                  (2 * SB_WIDTH, 3 * SB_WIDTH, "f32T+bf16"),
                  (3 * SB_WIDTH, 3 * SB_WIDTH + MEM_WIDTH, "bf16"))
_PLAN_A_SAMPLE = ((0, SB_WIDTH, "f32"), (SB_WIDTH, 2 * SB_WIDTH, "f32+bf16"),
                  (2 * SB_WIDTH, 3 * SB_WIDTH, "f32+bf16"),
                  (3 * SB_WIDTH, 3 * SB_WIDTH + MEM_WIDTH, "bf16"))
_PLAN_B_PROMPT = ((SGU_WIDTH, 2 * SGU_WIDTH, "gelu_norm_b"), (0, SGU_WIDTH, "gelu"),
                  (2 * SGU_WIDTH, 2 * SGU_WIDTH + MEM_WIDTH, "bf16"))
_PLAN_B_SAMPLE = ((SGU_WIDTH, 2 * SGU_WIDTH, "gelu_norm"),) + _PLAN_B_PROMPT[1:]
_PLAN_MEM = ((0, MEM_WIDTH, "f32T"), (MEM_WIDTH, 2 * MEM_WIDTH, "f32T"))


def kernel(x_prompt, x_sample, cache_sb_k, cache_sb_v, cache_mem_k, cache_mem_v,
           mem_prompt, g_mix, w_in_a, w_in_b, w_sp, b_sp, g_sgu, g_mem, w_mem_kv,
           w_out, g_ffn, w_gate, w_up, w_down, g_final):
    batch, seq, _ = x_prompt.shape
    dec_batch, dec_seq, _ = x_sample.shape
    depth = g_mix.shape[0]
    tm_p = 1024
    tm_gate = 1024
    tm_s = dec_batch * dec_seq

    y_p = x_prompt.reshape(batch * seq, D_MODEL)
    y_s = x_sample.reshape(tm_s, D_MODEL)
    mem_rows = mem_prompt.reshape(batch * N_MEM, D_MODEL)

    g_mix, g_mem, g_ffn, g_sgu = (_as_rows(g) for g in (g_mix, g_mem, g_ffn, g_sgu))
    w_in_a, w_in_b, w_mem_kv, w_out, w_gate, w_up, w_down = (
        w.astype(BF16) for w in (w_in_a, w_in_b, w_mem_kv, w_out, w_gate, w_up, w_down))

    sb_k_p, sb_v_p, sb_k_s, sb_v_s, mem_k_p, mem_v_p, sgu_v_s = ([] for _ in range(7))
    for l in range(depth):
        mkt_p, mvt_p = _norm_proj(mem_rows, (g_mem, l), (w_mem_kv, l), _PLAN_MEM,
                                  tm=N_MEM, rows_per_batch=N_MEM, name=f"mem_kv{l}")
        mem_k_p.append(_heads_minor(mkt_p, N_MEM_HEADS))
        mem_v_p.append(_heads_minor(mvt_p, N_MEM_HEADS))
        mkt_s, mvt_s = _heads_major(cache_mem_k[l]), _heads_major(cache_mem_v[l])

        if l % 2 == 0:
            ia = l // 2
            q, kt_f, k_b, vt_f, v_b, qm_p = _norm_proj(
                y_p, (g_mix, l), (w_in_a, ia), _PLAN_A_PROMPT, tm=tm_p,
                rows_per_batch=seq, name=f"in_a{l}_p")
            sb_k_p.append(_heads_minor(kt_f, N_SB_HEADS))
            sb_v_p.append(_heads_minor(vt_f, N_SB_HEADS))
            y_p = _sb_prompt_mix(y_p, q, k_b, v_b, qm_p, mkt_p, mvt_p, (w_out, l),
                                 batch=batch, seq=seq, name=f"sb_mix{l}_p")

            q, k_f, k_b, v_f, v_b, qm_s = _norm_proj(
                y_s, (g_mix, l), (w_in_a, ia), _PLAN_A_SAMPLE, tm=tm_s, name=f"in_a{l}_s")
            sb_k_s.append(k_f.reshape(dec_batch, dec_seq, N_SB_HEADS, HEAD_DIM))
            sb_v_s.append(v_f.reshape(dec_batch, dec_seq, N_SB_HEADS, HEAD_DIM))
            o_s = _sb_decode(q, k_b, v_b, _heads_major(cache_sb_k[ia]),
                             _heads_major(cache_sb_v[ia]),
                             batch=dec_batch, t_new=dec_seq)
        else:
            ib = l // 2
            span = min(seq, MLP_CHUNK)
            bias = jnp.repeat(b_sp[ib][:, :span].T, HEAD_DIM, axis=1)
            y_p = _gate_mix(y_p, (g_mix, l), (w_in_b, ib), (g_sgu, ib),
                            w_sp[ib][:, :span, :span], bias, mkt_p, mvt_p, (w_out, l),
                            tm=tm_gate, rows_per_batch=seq, name=f"gate_mix{l}_p")

            v_f, v_b, u, qm_s = _norm_proj(
                y_s, (g_mix, l), (w_in_b, ib), _PLAN_B_SAMPLE, tm=tm_s,
                gate_norm=(g_sgu, ib), name=f"in_b{l}_s")
            sgu_v_s.append(v_f.reshape(dec_batch, dec_seq, SGU_WIDTH))
            span = min(dec_seq, MLP_CHUNK)
            reps = MLP_CHUNK // span
            eye = jnp.eye(reps, dtype=F32)
            w_bd = jnp.einsum("ab,gts->gatbs", eye, w_sp[ib][:, :span, :span]).reshape(
                -1, MLP_CHUNK, MLP_CHUNK)
            bias = jnp.tile(jnp.repeat(b_sp[ib][:, :span].T, HEAD_DIM, axis=1), (reps, 1))
            o_s = _sgu(u, v_b, w_bd, bias, tm=tm_s, name=f"sgu{l}_s")

        last = g_final if l == depth - 1 else None
        ffn = (g_ffn, w_gate, w_up, w_down, l)
        y_p = _mix_ffn(y_p, ffn=ffn, g_final=last, tm=tm_p, name=f"ffn{l}_p")
        y_s = _mix_ffn(y_s, mix=(o_s, qm_s, mkt_s, mvt_s, w_out, l, dec_seq),
                       tm=dec_seq, name=f"mix_out{l}_s")
        y_s = _mix_ffn(y_s, ffn=ffn, g_final=last, tm=tm_s, name=f"ffn{l}_s")

    return (y_p.reshape(batch, seq, D_MODEL), y_s.reshape(dec_batch, dec_seq, D_MODEL),
            jnp.stack(sb_k_p), jnp.stack(sb_v_p), jnp.stack(sb_k_s), jnp.stack(sb_v_s),
            jnp.stack(mem_k_p), jnp.stack(mem_v_p), jnp.stack(sgu_v_s))
```

```python
import functools

import jax
import jax.numpy as jnp
from jax import lax
from jax.experimental import pallas as pl
from jax.experimental.pallas import tpu as pltpu

F32 = jnp.float32
BF16 = jnp.bfloat16

D_MODEL = 1024
HEAD_DIM = 64
SB_WIDTH = 768
N_SB_HEADS = SB_WIDTH // HEAD_DIM
SGU_WIDTH = 768
MLP_CHUNK = 128
MEM_WIDTH = 256
N_MEM = 256
N_MEM_HEADS = MEM_WIDTH // HEAD_DIM
D_FF = 2816
EPS = 1e-6
SB_SCALE = HEAD_DIM ** -0.5
LOG2E = 1.4426950408889634

LANES = 128
V7X_VMEM_BYTES = 64 * 1024 * 1024
VMEM_LIMIT = V7X_VMEM_BYTES - 8 * 1024 * 1024

SB_DEAD_MASS = 106.0

SB_TQ = 256
SB_TK = 256
SB_EARLY = 160
FF_CHUNK = 256
OUT_CHUNK = 256
OUT_DEPTH = 256
PROJ_CHUNK = 256


def _cparams(*semantics):
    return pltpu.CompilerParams(dimension_semantics=semantics,
                                vmem_limit_bytes=VMEM_LIMIT)


def _const_spec(shape):
    zeros = (0,) * len(shape)
    return pl.BlockSpec(shape, lambda *_: zeros, pipeline_mode=pl.Buffered(1))


def _layer_spec(stacked, layer):
    return pl.BlockSpec((None,) + stacked.shape[1:], lambda *_: (layer, 0, 0),
                        pipeline_mode=pl.Buffered(1))


def _as_rows(g):
    return g.reshape(g.shape[0], 1, g.shape[1])


def _rmsnorm_rows(x, g):
    ms = jnp.mean(x * x, axis=-1, keepdims=True)
    return (x * lax.rsqrt(ms + EPS)) * g


def _dot_nt(a, b):
    return lax.dot_general(a, b, (((1,), (1,)), ((), ())), preferred_element_type=F32)


_PLAN_FORMS = {
    "bf16": ("bf16",),
    "f32": ("f32",),
    "f32+bf16": ("f32", "bf16"),
    "f32T": ("f32T",),
    "f32T+bf16": ("f32T", "bf16"),
    "gelu": ("f32",),
    "gelu_norm": ("f32", "bf16"),
    "gelu_norm_b": ("bf16",),
}


def _ring_rows(x_hbm, x_buf, sems):
    i, n = pl.program_id(0), pl.num_programs(0)
    tm = x_buf.shape[1]

    def copy(step):
        slot = lax.rem(step, 3)
        return pltpu.make_async_copy(x_hbm.at[pl.ds(pl.multiple_of(step * tm, tm), tm), :],
                                     x_buf.at[slot], sems.at[slot])

    @pl.when(i == 0)
    def _():
        copy(i).start()

        @pl.when(n > 1)
        def _():
            copy(i + 1).start()

    @pl.when(i + 2 < n)
    def _():
        copy(i + 2).start()

    copy(i).wait()
    return x_buf[lax.rem(i, 3)]


def _norm_proj_kernel(x_ref, g_ref, w_ref, *rest, plan, has_gate_norm, ring):
    if ring:
        x, rest = _ring_rows(x_ref, *rest[-2:]), rest[:-2]
    else:
        x = x_ref[...]
    if has_gate_norm:
        gn_ref, out_refs = rest[0], rest[1:]
    else:
        gn_ref, out_refs = None, rest
    h = _rmsnorm_rows(x, g_ref[...]).astype(BF16)

    def store(refs, forms, a, b, value):
        for ref, form in zip(refs, forms):
            if form == "f32T":
                ref[a:b, :] = value.T
            else:
                ref[:, a:b] = value.astype(BF16 if form == "bf16" else F32)

    k = 0
    for lo, hi, kind in plan:
        forms = _PLAN_FORMS[kind]
        refs = out_refs[k:k + len(forms)]
        k += len(forms)
        acts = []
        for a in range(lo, hi, PROJ_CHUNK):
            b = min(a + PROJ_CHUNK, hi)
            p = jnp.dot(h, w_ref[:, a:b], preferred_element_type=F32)
            if kind.startswith("gelu"):
                p = jax.nn.gelu(p)
            if kind.startswith("gelu_norm"):
                acts.append((a - lo, b - lo, p))
            else:
                store(refs, forms, a - lo, b - lo, p)
        if acts:
            ms = sum(jnp.sum(t * t, axis=-1, keepdims=True) for _, _, t in acts) / (hi - lo)
            inv = lax.rsqrt(ms + EPS)
            for a0, b0, t in acts:
                store(refs, forms, a0, b0, (t * inv) * gn_ref[:, a0:b0])


def _norm_proj(x, g, w, plan, *, tm, rows_per_batch=None, gate_norm=None, ring=False,
               name):
    m, d = x.shape
    out_shape, out_specs = [], []
    for lo, hi, kind in plan:
        for form in _PLAN_FORMS[kind]:
            if form == "f32T":
                per = rows_per_batch // tm
                out_shape.append(jax.ShapeDtypeStruct(
                    (m // rows_per_batch, hi - lo, rows_per_batch), F32))
                out_specs.append(pl.BlockSpec(
                    (None, hi - lo, tm), lambda i, per=per: (i // per, 0, i % per)))
            else:
                out_shape.append(jax.ShapeDtypeStruct(
                    (m, hi - lo), BF16 if form == "bf16" else F32))
                out_specs.append(pl.BlockSpec((tm, hi - lo), lambda i: (i, 0)))
    layered = [g, w] + ([gate_norm] if gate_norm is not None else [])
    x_spec = (pl.BlockSpec(memory_space=pl.ANY) if ring
              else pl.BlockSpec((tm, d), lambda i: (i, 0)))
    in_specs = [x_spec] + [_layer_spec(arr, layer) for arr, layer in layered]
    args = [x] + [arr for arr, _ in layered]
    scratch = [pltpu.VMEM((3, tm, d), F32), pltpu.SemaphoreType.DMA((3,))] if ring else []
    return pl.pallas_call(
        functools.partial(_norm_proj_kernel, plan=plan,
                          has_gate_norm=gate_norm is not None, ring=ring),
        out_shape=out_shape, grid=(m // tm,), in_specs=in_specs,
        out_specs=out_specs, scratch_shapes=scratch,
        compiler_params=_cparams("arbitrary"), name=name,
    )(*args)


def _upper(n):
    j = lax.broadcasted_iota(jnp.int32, (n, n), 0)
    s = lax.broadcasted_iota(jnp.int32, (n, n), 1)
    return jnp.where(j >= s, 1.0, 0.0).astype(BF16)


def _sb_weights(z, upper, mass, causal):
    soft = jnp.maximum(z, 0.0) + jnp.log(1.0 + jnp.exp2(jnp.abs(z) * -LOG2E))
    if causal is not None:
        soft = jnp.where(causal, soft, 0.0)
    from_here = jnp.dot(soft.astype(BF16), upper, preferred_element_type=F32) + mass
    a = jnp.exp2((z - from_here) * LOG2E)
    if causal is not None:
        a = jnp.where(causal, a, 0.0)
    return a.astype(BF16), mass + jnp.sum(soft, axis=-1, keepdims=True)


def _sb_prompt_kernel(q_ref, k_ref, v_ref, y_ref, qm_ref, mk_ref, mv_ref, w_ref,
                      out_ref, mass_ref, *, tq, tk, early):
    i = pl.program_id(1)
    pairs = SB_WIDTH // LANES
    lane = lax.broadcasted_iota(jnp.int32, (1, LANES), 1)
    head0 = lane < HEAD_DIM
    upper = _upper(tk)
    mem = _mem_rows(_mem_scores(qm_ref, mk_ref), mv_ref)
    row = lax.broadcasted_iota(jnp.int32, (2 * tq, tk), 0) % tq
    col = lax.broadcasted_iota(jnp.int32, (2 * tq, tk), 1)
    has_prev = i > 0
    j_prev = jnp.maximum(i - 1, 0)

    def both_heads(x, lo, hi):
        return jnp.concatenate([x[lo:hi], x[tq + lo:tq + hi]], axis=0)

    def sweeper(p):
        lanes = slice(p * LANES, (p + 1) * LANES)
        q = q_ref[:, lanes] * jnp.asarray(SB_SCALE, BF16)
        zq = jnp.zeros_like(q)
        q2 = jnp.concatenate([jnp.where(head0, q, zq), jnp.where(head0, zq, q)], axis=0)

        def logits(qs, j):
            return _dot_nt(qs, k_ref[pl.ds(pl.multiple_of(j * tk, tk), tk), lanes])

        def values(a, j):
            n = a.shape[0] // 2
            vb = v_ref[pl.ds(pl.multiple_of(j * tk, tk), tk), lanes]
            zv = jnp.zeros_like(vb)
            return (jnp.dot(a[:n], jnp.where(head0, vb, zv), preferred_element_type=F32)
                    + jnp.dot(a[n:], jnp.where(head0, zv, vb), preferred_element_type=F32))

        return q2, logits, values

    late = tq - early
    least_early = least_late = jnp.asarray(jnp.inf, F32)
    mixed = []
    for p in range(pairs):
        q2, logits, values = sweeper(p)
        z_diag, z_prev = logits(q2, i), logits(both_heads(q2, 0, early), j_prev)
        a_diag, mass_diag = _sb_weights(z_diag, upper, jnp.zeros((2 * tq, 1), F32),
                                        col < row)
        mass_e = both_heads(mass_diag, 0, early)
        a_prev, mass_prev = _sb_weights(z_prev, upper, mass_e, None)
        acc = values(a_diag, i)
        acc_early = acc[:early] + jnp.where(has_prev, values(a_prev, j_prev), 0.0)
        mixed.append(jnp.concatenate([acc_early, acc[early:]], axis=0).astype(BF16))
        mass_e = jnp.where(has_prev, mass_prev, mass_e)
        mass_l = both_heads(mass_diag, early, tq)
        mass_ref[p] = jnp.concatenate(
            [mass_e[:early], mass_l[:late], mass_e[early:], mass_l[late:]], axis=0)
        least_early = jnp.minimum(least_early, jnp.min(mass_e))
        least_late = jnp.minimum(least_late, jnp.min(mass_l))
    out_ref[...] = _out_rows(y_ref, _mixer_part_by_depth(mixed, w_ref), mem, w_ref)

    late_row = lax.broadcasted_iota(jnp.int32, (2 * tq, 1), 0) % tq >= early

    def cond(state):
        j, least = state
        return jnp.logical_and(j >= 0, least < SB_DEAD_MASS)

    def body(state):
        j, _ = state
        fresh = jnp.logical_or(late_row, j != i - 1)
        least = jnp.asarray(jnp.inf, F32)
        for p in range(pairs):
            q2, logits, values = sweeper(p)
            mass_old = mass_ref[p]
            a, mass = _sb_weights(logits(q2, j), upper, mass_old, None)
            a = jnp.where(fresh, a, jnp.zeros_like(a))
            mass = jnp.where(fresh, mass, mass_old)
            out_ref[...] += jnp.dot(values(a, j).astype(BF16),
                                    w_ref[p * LANES:(p + 1) * LANES, :],
                                    preferred_element_type=F32)
            mass_ref[p] = mass
            least = jnp.minimum(least, jnp.min(mass))
        return j - 1, least

    j_start = jnp.where(least_late < SB_DEAD_MASS, i - 1, i - 2)
    lax.while_loop(cond, body, (j_start, jnp.minimum(least_early, least_late)))


def _sb_prompt_mix(y, q, k, v, qm, mem_kt, mem_vt, w_out, *, batch, seq, name):
    tq, tk = SB_TQ, SB_TK
    assert tq == tk and seq % tq == 0
    per = seq // tq
    q3, k3, v3 = (a.reshape(batch, seq, SB_WIDTH) for a in (q, k, v))
    q_spec = pl.BlockSpec((None, tq, SB_WIDTH), lambda b, i: (b, i, 0))
    kv_spec = pl.BlockSpec((None, seq, SB_WIDTH), lambda b, i: (b, 0, 0))
    row = lambda width: pl.BlockSpec((tq, width), lambda b, i: (b * per + i, 0))
    mem_spec = pl.BlockSpec((None, MEM_WIDTH, N_MEM), lambda b, i: (b, 0, 0))
    return pl.pallas_call(
        functools.partial(_sb_prompt_kernel, tq=tq, tk=tk, early=SB_EARLY),
        out_shape=jax.ShapeDtypeStruct((batch * seq, D_MODEL), F32),
        grid=(batch, per),
        in_specs=[q_spec, kv_spec, kv_spec, row(D_MODEL), row(MEM_WIDTH),
                  mem_spec, mem_spec, _layer_spec(*w_out)],
        out_specs=row(D_MODEL),
        scratch_shapes=[pltpu.VMEM((SB_WIDTH // LANES, 2 * tq, 1), F32)],
        compiler_params=_cparams("arbitrary", "arbitrary"),
        name=name,
    )(q3, k3, v3, y, qm, mem_kt, mem_vt, w_out[0])


def _sb_decode_kernel(q_ref, kn_ref, vn_ref, kc_hbm, vc_hbm, o_ref,
                      k_buf, v_buf, sems, acc_ref, *, t_new, tk, n_blocks):
    b = pl.program_id(0)
    n_rows = N_SB_HEADS * t_new
    last = n_blocks - 1

    def block_copies(stream, j, slot):
        keys = pl.ds(pl.multiple_of(j * tk, tk), tk)
        return (pltpu.make_async_copy(kc_hbm.at[stream, :, keys], k_buf.at[slot],
                                      sems.at[0, slot]),
                pltpu.make_async_copy(vc_hbm.at[stream, :, keys], v_buf.at[slot],
                                      sems.at[1, slot]))

    def fetch(stream, j, slot):
        for copy in block_copies(stream, j, slot):
            copy.start()

    def arrive(stream, j, slot):
        for copy in block_copies(stream, j, slot):
            copy.wait()

    @pl.when(b == 0)
    def _():
        fetch(b, last, 0)

    @pl.when(b + 1 < pl.num_programs(0))
    def _():
        fetch(b + 1, last, lax.rem(b + 1, 2))

    row_head = lax.broadcasted_iota(jnp.int32, (n_rows, SB_WIDTH), 0) // t_new
    col_head = lax.broadcasted_iota(jnp.int32, (n_rows, SB_WIDTH), 1) // HEAD_DIM
    q_rows = jnp.concatenate([q_ref[...]] * N_SB_HEADS, axis=0)
    q_bd = jnp.where(row_head == col_head, q_rows * SB_SCALE, 0.0).astype(BF16)

    n_pad = kn_ref.shape[0]
    q_idx = lax.broadcasted_iota(jnp.int32, (n_rows, n_pad), 0) % t_new
    s_idx = lax.broadcasted_iota(jnp.int32, (n_rows, n_pad), 1)
    a, mass = _sb_weights(_dot_nt(q_bd, kn_ref[...]), _upper(n_pad),
                          jnp.zeros((n_rows, 1), F32), s_idx < q_idx)
    acc_ref[...] = jnp.dot(a, vn_ref[...], preferred_element_type=F32)
    upper = _upper(tk)

    def sweep(slot, mass):
        z = jnp.dot(q_bd, k_buf[slot].astype(BF16), preferred_element_type=F32)
        a, mass = _sb_weights(z, upper, mass, None)
        acc_ref[...] += _dot_nt(a, v_buf[slot].astype(BF16))
        return mass

    own = lax.rem(b, 2)
    arrive(b, last, own)
    mass = sweep(own, mass)

    def cond(state):
        j, _, least = state
        return jnp.logical_and(j >= 0, least < SB_DEAD_MASS)

    def body(state):
        j, mass, _ = state
        fetch(b, j, 2)
        arrive(b, j, 2)
        mass = sweep(2, mass)
        return j - 1, mass, jnp.min(mass)

    lax.while_loop(cond, body, (last - 1, mass, jnp.min(mass)))

    out_head = lax.broadcasted_iota(jnp.int32, (t_new, SB_WIDTH), 1) // HEAD_DIM
    out = jnp.zeros((t_new, SB_WIDTH), F32)
    for h in range(N_SB_HEADS):
        out = out + jnp.where(out_head == h, acc_ref[h * t_new:(h + 1) * t_new, :], 0.0)
    o_ref[...] = out.astype(o_ref.dtype)


def _sb_decode(q, k_new, v_new, cache_kt, cache_vt, *, batch, t_new):
    past = cache_kt.shape[2]
    tk = SB_TK
    assert past % tk == 0
    pad = ((0, 0), (0, LANES - t_new), (0, 0))
    kn = jnp.pad(k_new.reshape(batch, t_new, SB_WIDTH), pad)
    vn = jnp.pad(v_new.reshape(batch, t_new, SB_WIDTH), pad)
    row_spec = pl.BlockSpec((None, t_new, SB_WIDTH), lambda b: (b, 0, 0))
    new_spec = pl.BlockSpec((None, LANES, SB_WIDTH), lambda b: (b, 0, 0))
    hbm_spec = pl.BlockSpec(memory_space=pl.ANY)
    out = pl.pallas_call(
        functools.partial(_sb_decode_kernel, t_new=t_new, tk=tk, n_blocks=past // tk),
        out_shape=jax.ShapeDtypeStruct((batch, t_new, SB_WIDTH), BF16),
        grid=(batch,),
        in_specs=[row_spec, new_spec, new_spec, hbm_spec, hbm_spec],
        out_specs=row_spec,
        scratch_shapes=[pltpu.VMEM((3, SB_WIDTH, tk), F32),
                        pltpu.VMEM((3, SB_WIDTH, tk), F32),
                        pltpu.SemaphoreType.DMA((2, 3)),
                        pltpu.VMEM((N_SB_HEADS * t_new, SB_WIDTH), F32)],
        compiler_params=_cparams("arbitrary"),
        name="sb_decode",
    )(q.reshape(batch, t_new, SB_WIDTH), kn, vn, cache_kt, cache_vt)
    return out.reshape(batch * t_new, SB_WIDTH)


def _heads_major(x):
    b, t, h, d = x.shape
    return jnp.transpose(x, (0, 2, 3, 1)).reshape(b, h * d, t)


def _heads_minor(xt, heads):
    b, hd, t = xt.shape
    return jnp.transpose(xt.reshape(b, heads, hd // heads, t), (0, 3, 1, 2))


def _sgu_kernel(u_ref, v_ref, w_ref, b_ref, o_ref, *, span, n_chunks):
    lane = lax.broadcasted_iota(jnp.int32, (1, LANES), 1)
    group0 = lane < HEAD_DIM
    t = lax.broadcasted_iota(jnp.int32, (span, span), 0)
    s = lax.broadcasted_iota(jnp.int32, (span, span), 1)
    tril = s <= t
    for p in range(SGU_WIDTH // LANES):
        w_pair = jnp.concatenate(
            [jnp.where(tril, w_ref[2 * p], 0.0), jnp.where(tril, w_ref[2 * p + 1], 0.0)],
            axis=1).astype(BF16)
        cols = slice(p * LANES, (p + 1) * LANES)
        for c in range(n_chunks):
            rows = slice(c * span, (c + 1) * span)
            vv = v_ref[rows, cols]
            zv = jnp.zeros_like(vv)
            v_pair = jnp.concatenate([jnp.where(group0, vv, zv),
                                      jnp.where(group0, zv, vv)], axis=0)
            mixed = jnp.dot(w_pair, v_pair, preferred_element_type=F32) + b_ref[:, cols]
            o_ref[rows, cols] = (u_ref[rows, cols] * mixed).astype(o_ref.dtype)


def _sgu(u, v, w, bias, *, tm, name):
    m = u.shape[0]
    span = w.shape[-1]
    blk = pl.BlockSpec((tm, SGU_WIDTH), lambda i: (i, 0))
    return pl.pallas_call(
        functools.partial(_sgu_kernel, span=span, n_chunks=tm // span),
        out_shape=jax.ShapeDtypeStruct((m, SGU_WIDTH), BF16),
        grid=(m // tm,),
        in_specs=[blk, blk, _const_spec(w.shape), _const_spec(bias.shape)],
        out_specs=blk, compiler_params=_cparams("arbitrary"), name=name,
    )(u, v, w, bias)


_OUT_CHUNKS = [slice(c, c + OUT_CHUNK) for c in range(0, D_MODEL, OUT_CHUNK)]


def _mem_scores(qm_ref, mk_ref):
    lane = lax.broadcasted_iota(jnp.int32, (1, MEM_WIDTH), 1) // HEAD_DIM
    qm = qm_ref[...] * jnp.asarray(SB_SCALE, BF16)
    zq = jnp.zeros_like(qm)
    mk = mk_ref[...].astype(BF16)
    return [jnp.dot(jnp.where(lane == h, qm, zq), mk, preferred_element_type=F32)
            for h in range(N_MEM_HEADS)]


def _mem_rows(scores, mv_ref):
    sublane = lax.broadcasted_iota(jnp.int32, (MEM_WIDTH, 1), 0) // HEAD_DIM
    mv = mv_ref[...].astype(BF16)
    zv = jnp.zeros_like(mv)
    mem = 0.0
    for h, s in enumerate(scores):
        e = jnp.exp(s - jnp.max(s, axis=-1, keepdims=True))
        p = e / jnp.sum(e, axis=-1, keepdims=True)
        mem = mem + _dot_nt(p.astype(BF16), jnp.where(sublane == h, mv, zv))
    return mem.astype(BF16)


def _mixer_part(o, w_ref):
    return [jnp.dot(o, w_ref[:o.shape[1], cols], preferred_element_type=F32)
            for cols in _OUT_CHUNKS]


def _mixer_part_by_depth(blocks, w_ref):
    per = OUT_DEPTH // blocks[0].shape[1]
    total = None
    for g in range(len(blocks) // per):
        o = jnp.concatenate(blocks[g * per:(g + 1) * per], axis=1)
        rows = slice(g * OUT_DEPTH, (g + 1) * OUT_DEPTH)
        share = [jnp.dot(o, w_ref[rows, cols], preferred_element_type=F32)
                 for cols in _OUT_CHUNKS]
        total = share if total is None else [s + t for s, t in zip(total, share)]
    return total


def _out_rows(y_ref, mixer_part, mem, w_ref):
    width = D_MODEL - mem.shape[1]
    return jnp.concatenate(
        [(y_ref[:, cols] + part) + jnp.dot(mem, w_ref[width:, cols],
                                           preferred_element_type=F32)
         for cols, part in zip(_OUT_CHUNKS, mixer_part)], axis=1)


def _mix_rows(y_ref, o_ref, qm_ref, mk_ref, mv_ref, w_ref):
    scores = _mem_scores(qm_ref, mk_ref)
    mixer_part = _mixer_part(o_ref[...], w_ref)
    return _out_rows(y_ref, mixer_part, _mem_rows(scores, mv_ref), w_ref)


def _gate_mix_kernel(y_ref, g_ref, w_ref, gn_ref, wsp_ref, b_ref, mk_ref, mv_ref,
                     wout_ref, out_ref, o_scr, *, span):
    h = _rmsnorm_rows(y_ref[...], g_ref[...]).astype(BF16)

    def proj(lo, hi):
        return [jnp.dot(h, w_ref[:, a:min(a + PROJ_CHUNK, hi)], preferred_element_type=F32)
                for a in range(lo, hi, PROJ_CHUNK)]

    acts = [jax.nn.gelu(p) for p in proj(SGU_WIDTH, 2 * SGU_WIDTH)]
    ms = sum(jnp.sum(t * t, axis=-1, keepdims=True) for t in acts) / SGU_WIDTH
    v = (jnp.concatenate(acts, axis=1) * lax.rsqrt(ms + EPS)) * gn_ref[...]
    u = jnp.concatenate([jax.nn.gelu(p) for p in proj(0, SGU_WIDTH)], axis=1)
    (qm,) = proj(2 * SGU_WIDTH, 2 * SGU_WIDTH + MEM_WIDTH)
    _sgu_kernel(u, v.astype(BF16), wsp_ref, b_ref, o_scr, span=span,
                n_chunks=y_ref.shape[0] // span)
    out_ref[...] = _mix_rows(y_ref, o_scr, qm.astype(BF16), mk_ref, mv_ref, wout_ref)


def _gate_mix(y, g, w_in, gate_norm, w_sp, bias, mem_kt, mem_vt, w_out, *,
              tm, rows_per_batch, name):
    m = y.shape[0]
    span = w_sp.shape[-1]
    assert rows_per_batch % tm == 0 and tm % span == 0
    per = rows_per_batch // tm
    row = pl.BlockSpec((tm, D_MODEL), lambda i: (i, 0))
    mem_spec = pl.BlockSpec((None, MEM_WIDTH, N_MEM), lambda i: (i // per, 0, 0))
    layered = [g, w_in, gate_norm]
    return pl.pallas_call(
        functools.partial(_gate_mix_kernel, span=span),
        out_shape=jax.ShapeDtypeStruct((m, D_MODEL), F32),
        grid=(m // tm,),
        in_specs=[row] + [_layer_spec(*pair) for pair in layered] + [
            _const_spec(w_sp.shape), _const_spec(bias.shape), mem_spec, mem_spec,
            _layer_spec(*w_out)],
        out_specs=row,
        scratch_shapes=[pltpu.VMEM((tm, SGU_WIDTH), BF16)],
        compiler_params=_cparams("arbitrary"), name=name,
    )(y, *[arr for arr, _ in layered], w_sp, bias, mem_kt, mem_vt, w_out[0])


def _ffn_rows(y, g_ref, wg_ref, wu_ref, wd_ref, gf_ref):
    h = _rmsnorm_rows(y, g_ref[...]).astype(BF16)
    acts = []
    for c in range(D_FF // FF_CHUNK):
        cols = slice(c * FF_CHUNK, (c + 1) * FF_CHUNK)
        gate = jnp.dot(h, wg_ref[:, cols], preferred_element_type=F32)
        up = jnp.dot(h, wu_ref[:, cols], preferred_element_type=F32)
        acts.append((jax.nn.silu(gate) * up).astype(BF16))
    acc = y + jnp.dot(jnp.concatenate(acts, axis=1), wd_ref[...],
                      preferred_element_type=F32)
    return acc if gf_ref is None else _rmsnorm_rows(acc, gf_ref[...])


def _mix_ffn_kernel(*refs, mix, ffn, final):
    refs = list(refs)
    out_ref = refs.pop()
    if mix:
        y = _mix_rows(*refs[:6])
        del refs[:6]
    else:
        y = refs.pop(0)[...]
    if ffn:
        y = _ffn_rows(y, *refs[:4], refs[4] if final else None)
    out_ref[...] = y


def _mix_ffn(y, *, mix=None, ffn=None, g_final=None, tm, name):
    m = y.shape[0]
    row = pl.BlockSpec((tm, D_MODEL), lambda i: (i, 0))
    args, in_specs = [y], [row]
    if mix is not None:
        o, qm, mem_kt, mem_vt, w_out, layer, rows_per_batch = mix
        assert rows_per_batch % tm == 0
        per = rows_per_batch // tm
        mem_spec = pl.BlockSpec((None, MEM_WIDTH, N_MEM), lambda i: (i // per, 0, 0))
        args += [o, qm, mem_kt, mem_vt, w_out]
        in_specs += [pl.BlockSpec((tm, SB_WIDTH), lambda i: (i, 0)),
                     pl.BlockSpec((tm, MEM_WIDTH), lambda i: (i, 0)),
                     mem_spec, mem_spec, _layer_spec(w_out, layer)]
    if ffn is not None:
        *stacked, layer = ffn
        args += stacked
        in_specs += [_layer_spec(a, layer) for a in stacked]
        if g_final is not None:
            args.append(g_final.reshape(1, D_MODEL))
            in_specs.append(_const_spec((1, D_MODEL)))
    return pl.pallas_call(
        functools.partial(_mix_ffn_kernel, mix=mix is not None, ffn=ffn is not None,
                          final=g_final is not None),
        out_shape=jax.ShapeDtypeStruct((m, D_MODEL), F32),
        grid=(m // tm,), in_specs=in_specs, out_specs=row,
        compiler_params=_cparams("arbitrary"), name=name,
    )(*args)


_PLAN_A_PROMPT = ((0, SB_WIDTH, "bf16"), (SB_WIDTH, 2 * SB_WIDTH, "f32T+bf16"),
                  (2 * SB_WIDTH, 3 * SB_WIDTH, "f32T+bf16"),
                  (3 * SB_WIDTH, 3 * SB_WIDTH + MEM_WIDTH, "bf16"))
_PLAN_A_SAMPLE = ((0, SB_WIDTH, "f32"), (SB_WIDTH, 2 * SB_WIDTH, "f32+bf16"),
                  (2 * SB_WIDTH, 3 * SB_WIDTH, "f32+bf16"),
                  (3 * SB_WIDTH, 3 * SB_WIDTH + MEM_WIDTH, "bf16"))
_PLAN_B_PROMPT = ((SGU_WIDTH, 2 * SGU_WIDTH, "gelu_norm_b"), (0, SGU_WIDTH, "gelu"),
                  (2 * SGU_WIDTH, 2 * SGU_WIDTH + MEM_WIDTH, "bf16"))
_PLAN_B_SAMPLE = ((SGU_WIDTH, 2 * SGU_WIDTH, "gelu_norm"),) + _PLAN_B_PROMPT[1:]
_PLAN_MEM = ((0, MEM_WIDTH, "f32T"), (MEM_WIDTH, 2 * MEM_WIDTH, "f32T"))


def kernel(x_prompt, x_sample, cache_sb_k, cache_sb_v, cache_mem_k, cache_mem_v,
           mem_prompt, g_mix, w_in_a, w_in_b, w_sp, b_sp, g_sgu, g_mem, w_mem_kv,
           w_out, g_ffn, w_gate, w_up, w_down, g_final):
    batch, seq, _ = x_prompt.shape
    dec_batch, dec_seq, _ = x_sample.shape
    depth = g_mix.shape[0]
    tm_p = 1024
    tm_gate = 1024
    tm_s = dec_batch * dec_seq

    y_p = x_prompt.reshape(batch * seq, D_MODEL)
    y_s = x_sample.reshape(tm_s, D_MODEL)
    mem_rows = mem_prompt.reshape(batch * N_MEM, D_MODEL)

    g_mix, g_mem, g_ffn, g_sgu = (_as_rows(g) for g in (g_mix, g_mem, g_ffn, g_sgu))
    w_in_a, w_in_b, w_mem_kv, w_out, w_gate, w_up, w_down = (
        w.astype(BF16) for w in (w_in_a, w_in_b, w_mem_kv, w_out, w_gate, w_up, w_down))

    sb_k_p, sb_v_p, sb_k_s, sb_v_s, mem_k_p, mem_v_p, sgu_v_s = ([] for _ in range(7))
    for l in range(depth):
        mkt_p, mvt_p = _norm_proj(mem_rows, (g_mem, l), (w_mem_kv, l), _PLAN_MEM,
                                  tm=N_MEM, rows_per_batch=N_MEM, name=f"mem_kv{l}")
        mem_k_p.append(_heads_minor(mkt_p, N_MEM_HEADS))
        mem_v_p.append(_heads_minor(mvt_p, N_MEM_HEADS))
        mkt_s, mvt_s = _heads_major(cache_mem_k[l]), _heads_major(cache_mem_v[l])

        if l % 2 == 0:
            ia = l // 2
            q, kt_f, k_b, vt_f, v_b, qm_p = _norm_proj(
                y_p, (g_mix, l), (w_in_a, ia), _PLAN_A_PROMPT, tm=tm_p,
                rows_per_batch=seq, ring=True, name=f"in_a{l}_p")
            sb_k_p.append(_heads_minor(kt_f, N_SB_HEADS))
            sb_v_p.append(_heads_minor(vt_f, N_SB_HEADS))
            y_p = _sb_prompt_mix(y_p, q, k_b, v_b, qm_p, mkt_p, mvt_p, (w_out, l),
                                 batch=batch, seq=seq, name=f"sb_mix{l}_p")

            q, k_f, k_b, v_f, v_b, qm_s = _norm_proj(
                y_s, (g_mix, l), (w_in_a, ia), _PLAN_A_SAMPLE, tm=tm_s, name=f"in_a{l}_s")
            sb_k_s.append(k_f.reshape(dec_batch, dec_seq, N_SB_HEADS, HEAD_DIM))
            sb_v_s.append(v_f.reshape(dec_batch, dec_seq, N_SB_HEADS, HEAD_DIM))
            o_s = _sb_decode(q, k_b, v_b, _heads_major(cache_sb_k[ia]),
                             _heads_major(cache_sb_v[ia]),
                             batch=dec_batch, t_new=dec_seq)
        else:
            ib = l // 2
            span = min(seq, MLP_CHUNK)
            bias = jnp.repeat(b_sp[ib][:, :span].T, HEAD_DIM, axis=1)
            y_p = _gate_mix(y_p, (g_mix, l), (w_in_b, ib), (g_sgu, ib),
                            w_sp[ib][:, :span, :span], bias, mkt_p, mvt_p, (w_out, l),
                            tm=tm_gate, rows_per_batch=seq, name=f"gate_mix{l}_p")

            v_f, v_b, u, qm_s = _norm_proj(
                y_s, (g_mix, l), (w_in_b, ib), _PLAN_B_SAMPLE, tm=tm_s,
                gate_norm=(g_sgu, ib), name=f"in_b{l}_s")
            sgu_v_s.append(v_f.reshape(dec_batch, dec_seq, SGU_WIDTH))
            span = min(dec_seq, MLP_CHUNK)
            reps = MLP_CHUNK // span
            eye = jnp.eye(reps, dtype=F32)
            w_bd = jnp.einsum("ab,gts->gatbs", eye, w_sp[ib][:, :span, :span]).reshape(
                -1, MLP_CHUNK, MLP_CHUNK)
            bias = jnp.tile(jnp.repeat(b_sp[ib][:, :span].T, HEAD_DIM, axis=1), (reps, 1))
            o_s = _sgu(u, v_b, w_bd, bias, tm=tm_s, name=f"sgu{l}_s")

        last = g_final if l == depth - 1 else None
        ffn = (g_ffn, w_gate, w_up, w_down, l)
        y_p = _mix_ffn(y_p, ffn=ffn, g_final=last, tm=tm_p, name=f"ffn{l}_p")
        y_s = _mix_ffn(y_s, mix=(o_s, qm_s, mkt_s, mvt_s, w_out, l, dec_seq),
                       tm=dec_seq, name=f"mix_out{l}_s")
        y_s = _mix_ffn(y_s, ffn=ffn, g_final=last, tm=tm_s, name=f"ffn{l}_s")

    return (y_p.reshape(batch, seq, D_MODEL), y_s.reshape(dec_batch, dec_seq, D_MODEL),
            jnp.stack(sb_k_p), jnp.stack(sb_v_p), jnp.stack(sb_k_s), jnp.stack(sb_v_s),
            jnp.stack(mem_k_p), jnp.stack(mem_v_p), jnp.stack(sgu_v_s))
```
